```python
import math
import jax
import jax.numpy as jnp
from jax import lax
import numpy as np

D_MODEL = 2048
BATCH = 1
SEQ = 8192
DEPTH = 1

SSD_EXPAND = 2
SSD_D_INNER = SSD_EXPAND * D_MODEL
SSD_HEAD_DIM = 64
SSD_N_HEADS = SSD_D_INNER // SSD_HEAD_DIM
SSD_N_GROUPS = 8
SSD_D_STATE = 128
SSD_CONV_WIDTH = 4
SSD_CHUNK = 128
SSD_XBC = SSD_D_INNER + 2 * SSD_N_GROUPS * SSD_D_STATE

ATTN_HEAD_DIM = 64
ATTN_N_Q_HEADS = D_MODEL // ATTN_HEAD_DIM
ATTN_N_KV_HEADS = ATTN_N_Q_HEADS // 8
ATTN_Q_WIDTH = ATTN_N_Q_HEADS * ATTN_HEAD_DIM
ATTN_KV_WIDTH = ATTN_N_KV_HEADS * ATTN_HEAD_DIM
ATTN_WINDOW = 128
ROPE_THETA = 10000.0

FFN_HIDDEN = 5632

DEEPNORM_ALPHA = (2 * DEPTH) ** 0.25
DEEPNORM_BETA = (8 * DEPTH) ** -0.25
LN_EPS = 1e-5
RMS_EPS = 1e-5

PROJ_SIZES = (SSD_D_INNER, SSD_XBC, SSD_N_HEADS, ATTN_Q_WIDTH, ATTN_KV_WIDTH, ATTN_KV_WIDTH, D_MODEL, D_MODEL)
PROJ_SPLITS = tuple(int(s) for s in np.cumsum(PROJ_SIZES)[:-1])
PROJ_WIDTH = int(sum(PROJ_SIZES))

kernel_name = 'hybrid_ssd_swa_sink_macaron_deepnorm'


def layer_norm(x, g, b):
    xf = x.astype(jnp.float32)
    mu = jnp.mean(xf, axis=-1, keepdims=True)
    xc = xf - mu
    var = jnp.mean(xc * xc, axis=-1, keepdims=True)
    return (xc * lax.rsqrt(var + LN_EPS) * g + b).astype(x.dtype)


def swiglu(x, w_gate, w_up, w_down):
    return (jax.nn.silu(x @ w_gate) * (x @ w_up)) @ w_down


def causal_depthwise_conv(u, w, b):
    k_width, ch = w.shape
    y = lax.conv_general_dilated(
        u, w[:, None, :], window_strides=(1,), padding=[(k_width - 1, 0)],
        dimension_numbers=('NWC', 'WIO', 'NWC'), feature_group_count=ch)
    return y + b


def ssd_chunked(x, dt, a, bmat, cmat):
    bsz, t_len, n_heads, p_dim = x.shape
    n_groups, n_state = bmat.shape[2], bmat.shape[3]
    r = n_heads // n_groups
    q_len = SSD_CHUNK
    n_chunks = t_len // q_len
    xd = (x * dt[..., None]).reshape(bsz, n_chunks, q_len, n_groups, r, p_dim)
    adt = (dt * a).reshape(bsz, n_chunks, q_len, n_groups, r).transpose(0, 1, 3, 4, 2)
    bc = bmat.reshape(bsz, n_chunks, q_len, n_groups, n_state)
    cc = cmat.reshape(bsz, n_chunks, q_len, n_groups, n_state)
    a_cs = jnp.cumsum(adt, axis=-1)
    causal = jnp.tril(jnp.ones((q_len, q_len), dtype=bool))
    seg = jnp.exp(jnp.where(causal, a_cs[..., :, None] - a_cs[..., None, :], -jnp.inf))
    cb = jnp.einsum('bclgn,bcsgn->bcgls', cc, bc)
    y_diag = jnp.einsum('bcgrls,bcsgrp->bclgrp', cb[:, :, :, None] * seg, xd)
    decay_to_end = jnp.exp(a_cs[..., -1:] - a_cs).transpose(0, 1, 4, 2, 3)
    states = jnp.einsum('bclgn,bclgrp->bcgrpn', bc, xd * decay_to_end[..., None])
    chunk_decay = jnp.exp(a_cs[..., -1])

    def step(h, inp):
        s_c, d_c = inp
        return h * d_c[..., None, None] + s_c, h

    h0 = jnp.zeros((bsz, n_groups, r, p_dim, n_state), dtype=states.dtype)
    _, prev = lax.scan(step, h0, (jnp.moveaxis(states, 1, 0), jnp.moveaxis(chunk_decay, 1, 0)))
    prev = jnp.moveaxis(prev, 0, 1)
    decay_in = jnp.exp(a_cs).transpose(0, 1, 4, 2, 3)
    y_off = jnp.einsum('bclgn,bcgrpn->bclgrp', cc, prev) * decay_in[..., None]
    return (y_diag + y_off).reshape(bsz, t_len, n_heads, p_dim)


def ssd_branch(z, xbc, dt_raw, conv_w, conv_b, dt_bias, a_log, d_skip, norm_g, w_o):
    bsz, t_len, _ = z.shape
    xbc = jax.nn.silu(causal_depthwise_conv(xbc, conv_w, conv_b))
    gn = SSD_N_GROUPS * SSD_D_STATE
    xs = xbc[..., :SSD_D_INNER].reshape(bsz, t_len, SSD_N_HEADS, SSD_HEAD_DIM).astype(jnp.float32)
    bm = xbc[..., SSD_D_INNER:SSD_D_INNER + gn].reshape(bsz, t_len, SSD_N_GROUPS, SSD_D_STATE).astype(jnp.float32)
    cm = xbc[..., SSD_D_INNER + gn:].reshape(bsz, t_len, SSD_N_GROUPS, SSD_D_STATE).astype(jnp.float32)
    dt = jax.nn.softplus(dt_raw.astype(jnp.float32) + dt_bias.astype(jnp.float32))
    a = -jnp.exp(a_log.astype(jnp.float32))
    y = ssd_chunked(xs, dt, a, bm, cm) + d_skip.astype(jnp.float32)[:, None] * xs
    y = y.reshape(bsz, t_len, SSD_D_INNER) * jax.nn.silu(z.astype(jnp.float32))
    yg = y.reshape(bsz, t_len, SSD_N_GROUPS, SSD_D_INNER // SSD_N_GROUPS)
    yg = yg * lax.rsqrt(jnp.mean(yg * yg, axis=-1, keepdims=True) + RMS_EPS)
    y = yg.reshape(bsz, t_len, SSD_D_INNER) * norm_g.astype(jnp.float32)
    return y.astype(z.dtype) @ w_o


def rope(u, positions):
    dh = u.shape[-1]
    half = dh // 2
    inv_freq = ROPE_THETA ** (-jnp.arange(half, dtype=jnp.float32) * 2.0 / dh)
    ang = positions.astype(jnp.float32)[..., None] * inv_freq
    cos = jnp.cos(ang)[:, :, None, :]
    sin = jnp.sin(ang)[:, :, None, :]
    uf = u.astype(jnp.float32)
    u1, u2 = uf[..., :half], uf[..., half:]
    return jnp.concatenate([u1 * cos - u2 * sin, u2 * cos + u1 * sin], axis=-1).astype(u.dtype)


def sliding_window_attention(q, k, v, sinks):
    bsz, t_len, hq, dh = q.shape
    hkv = k.shape[2]
    g = hq // hkv
    w = ATTN_WINDOW
    nb = t_len // w
    qb = q.reshape(bsz, nb, w, hkv, g, dh).astype(jnp.float32)
    kb = k.reshape(bsz, nb, w, hkv, dh).astype(jnp.float32)
    vb = v.reshape(bsz, nb, w, hkv, dh).astype(jnp.float32)

    def with_prev(u):
        prev = jnp.pad(u[:, :-1], ((0, 0), (1, 0), (0, 0), (0, 0), (0, 0)))
        return jnp.concatenate([prev, u], axis=2)

    kk = with_prev(kb)
    vv = with_prev(vb)
    s = jnp.einsum('bnqhgd,bnkhd->bnhgqk', qb, kk) * (dh ** -0.5)
    qi = jnp.arange(w)[:, None]
    kj = jnp.arange(2 * w)[None, :]
    band = (kj > qi) & (kj <= qi + w)
    valid_prev = (jnp.arange(nb)[:, None, None] > 0) | (kj >= w)[None]
    mask = band[None] & valid_prev
    s = jnp.where(mask[None, :, None, None], s, -jnp.inf)
    sink = sinks.astype(jnp.float32).reshape(hkv, g)[None, None, :, :, None, None]
    m = jnp.maximum(jnp.max(s, axis=-1, keepdims=True), sink)
    p = jnp.exp(s - m)
    denom = jnp.sum(p, axis=-1, keepdims=True) + jnp.exp(sink - m)
    o = jnp.einsum('bnhgqk,bnkhd->bnqhgd', p / denom, vv)
    return o.reshape(bsz, t_len, hq * dh).astype(q.dtype)


def hybrid_mixer(h, positions, w_in, conv_w, conv_b, dt_bias, a_log, d_skip, ssd_norm_g,
                 w_ssd_o, attn_sinks, w_attn_o, w_out):
    bsz, t_len, _ = h.shape
    proj = h @ w_in
    z, xbc, dt_raw, q, k, v, gate_s, gate_a = jnp.split(proj, PROJ_SPLITS, axis=-1)
    y_s = ssd_branch(z, xbc, dt_raw, conv_w, conv_b, dt_bias, a_log, d_skip, ssd_norm_g, w_ssd_o)
    q = rope(q.reshape(bsz, t_len, ATTN_N_Q_HEADS, ATTN_HEAD_DIM), positions)
    k = rope(k.reshape(bsz, t_len, ATTN_N_KV_HEADS, ATTN_HEAD_DIM), positions)
    v = v.reshape(bsz, t_len, ATTN_N_KV_HEADS, ATTN_HEAD_DIM)
    y_a = sliding_window_attention(q, k, v, attn_sinks) @ w_attn_o
    merged = jax.nn.sigmoid(gate_s) * y_s + jax.nn.sigmoid(gate_a) * y_a
    return merged @ w_out


def setup_inputs(seed: int = 0) -> dict:
    key = jax.random.key(seed)
    ks = jax.random.split(key, 24)
    f32 = jnp.float32
    nl = DEPTH

    def nrm(k, shape, scale):
        return jax.random.normal(k, shape, f32) * scale

    x = nrm(ks[0], (BATCH, SEQ, D_MODEL), 1.0)
    positions = jnp.broadcast_to(jnp.arange(SEQ, dtype=jnp.int32), (BATCH, SEQ))
    ffn1_w_gate = nrm(ks[1], (nl, D_MODEL, FFN_HIDDEN), D_MODEL ** -0.5)
    ffn1_w_up = nrm(ks[2], (nl, D_MODEL, FFN_HIDDEN), D_MODEL ** -0.5)
    ffn1_w_down = nrm(ks[3], (nl, FFN_HIDDEN, D_MODEL), DEEPNORM_BETA * FFN_HIDDEN ** -0.5)
    ln1_g = 1.0 + nrm(ks[4], (nl, D_MODEL), 0.02)
    ln1_b = nrm(ks[5], (nl, D_MODEL), 0.02)
    w_in = nrm(ks[6], (nl, D_MODEL, PROJ_WIDTH), D_MODEL ** -0.5)
    conv_w = nrm(ks[7], (nl, SSD_CONV_WIDTH, SSD_XBC), SSD_CONV_WIDTH ** -0.5)
    conv_b = nrm(ks[8], (nl, SSD_XBC), 0.02)
    dt0 = jnp.exp(jax.random.uniform(ks[9], (nl, SSD_N_HEADS), f32, math.log(1e-3), math.log(1e-1)))
    dt_bias = dt0 + jnp.log(-jnp.expm1(-dt0))
    a_log = jnp.log(jax.random.uniform(ks[10], (nl, SSD_N_HEADS), f32, 1.0, 16.0))
    d_skip = 1.0 + nrm(ks[11], (nl, SSD_N_HEADS), 0.1)
    ssd_norm_g = 1.0 + nrm(ks[12], (nl, SSD_D_INNER), 0.02)
    w_ssd_o = nrm(ks[13], (nl, SSD_D_INNER, D_MODEL), SSD_D_INNER ** -0.5)
    attn_sinks = nrm(ks[14], (nl, ATTN_N_Q_HEADS), 0.5)
    w_attn_o = nrm(ks[15], (nl, ATTN_Q_WIDTH, D_MODEL), ATTN_Q_WIDTH ** -0.5)
    w_out = nrm(ks[16], (nl, D_MODEL, D_MODEL), DEEPNORM_BETA * D_MODEL ** -0.5)
    ln2_g = 1.0 + nrm(ks[17], (nl, D_MODEL), 0.02)
    ln2_b = nrm(ks[18], (nl, D_MODEL), 0.02)
    ffn2_w_gate = nrm(ks[19], (nl, D_MODEL, FFN_HIDDEN), D_MODEL ** -0.5)
    ffn2_w_up = nrm(ks[20], (nl, D_MODEL, FFN_HIDDEN), D_MODEL ** -0.5)
    ffn2_w_down = nrm(ks[21], (nl, FFN_HIDDEN, D_MODEL), DEEPNORM_BETA * FFN_HIDDEN ** -0.5)
    ln3_g = 1.0 + nrm(ks[22], (nl, D_MODEL), 0.02)
    ln3_b = nrm(ks[23], (nl, D_MODEL), 0.02)
    return {'x': x, 'positions': positions,
            'ffn1_w_gate': ffn1_w_gate, 'ffn1_w_up': ffn1_w_up, 'ffn1_w_down': ffn1_w_down,
            'ln1_g': ln1_g, 'ln1_b': ln1_b,
            'w_in': w_in, 'conv_w': conv_w, 'conv_b': conv_b, 'dt_bias': dt_bias, 'a_log': a_log,
            'd_skip': d_skip, 'ssd_norm_g': ssd_norm_g, 'w_ssd_o': w_ssd_o,
            'attn_sinks': attn_sinks, 'w_attn_o': w_attn_o, 'w_out': w_out,
            'ln2_g': ln2_g, 'ln2_b': ln2_b,
            'ffn2_w_gate': ffn2_w_gate, 'ffn2_w_up': ffn2_w_up, 'ffn2_w_down': ffn2_w_down,
            'ln3_g': ln3_g, 'ln3_b': ln3_b}


def reference(x, positions, ffn1_w_gate, ffn1_w_up, ffn1_w_down, ln1_g, ln1_b,
              w_in, conv_w, conv_b, dt_bias, a_log, d_skip, ssd_norm_g, w_ssd_o,
              attn_sinks, w_attn_o, w_out, ln2_g, ln2_b,
              ffn2_w_gate, ffn2_w_up, ffn2_w_down, ln3_g, ln3_b):
    h = x
    for l in range(DEPTH):
        h = layer_norm(DEEPNORM_ALPHA * h + 0.5 * swiglu(h, ffn1_w_gate[l], ffn1_w_up[l], ffn1_w_down[l]),
                       ln1_g[l], ln1_b[l])
        mix = hybrid_mixer(h, positions, w_in[l], conv_w[l], conv_b[l], dt_bias[l], a_log[l], d_skip[l],
                           ssd_norm_g[l], w_ssd_o[l], attn_sinks[l], w_attn_o[l], w_out[l])
        h = layer_norm(DEEPNORM_ALPHA * h + mix, ln2_g[l], ln2_b[l])
        h = layer_norm(DEEPNORM_ALPHA * h + 0.5 * swiglu(h, ffn2_w_gate[l], ffn2_w_up[l], ffn2_w_down[l]),
                       ln3_g[l], ln3_b[l])
    return h
```

```python
import functools
import math

import jax
import jax.numpy as jnp
from jax import lax
from jax.experimental import pallas as pl
from jax.experimental.pallas import tpu as pltpu

F32 = jnp.float32
BF16 = jnp.bfloat16

V7X_LANES = 128
V7X_SUBLANES = 8
V7X_VMEM_BYTES = 64 * 1024 * 1024

SSD_HEAD_DIM = 64
SSD_N_GROUPS = 8
SSD_D_STATE = 128
SSD_CONV_WIDTH = 4
SSD_CHUNK = 128
ATTN_HEAD_DIM = 64
ATTN_GROUP = 8
ATTN_WINDOW = 128
ROPE_THETA = 10000.0
DEPTH = 1
DEEPNORM_ALPHA = (2 * DEPTH) ** 0.25
LN_EPS = 1e-5
RMS_EPS = 1e-5


def _vmem_limit(block_bytes):
    return int(min(block_bytes * 5 // 4 + (8 << 20), V7X_VMEM_BYTES - (6 << 20)))


def _dot(a, b):
    return jnp.dot(a, b, preferred_element_type=F32)


def _dot_nt(a, b):
    return lax.dot_general(a, b, (((1,), (1,)), ((), ())), preferred_element_type=F32)


def _dot_tn(a, b):
    return lax.dot_general(a, b, (((0,), (0,)), ((), ())), preferred_element_type=F32)


def _dot_f32(a, b):
    return jnp.dot(a, b, preferred_element_type=F32, precision=lax.Precision.HIGHEST)


def _layer_norm(y, g, b):
    mu = jnp.mean(y, axis=-1, keepdims=True)
    yc = y - mu
    var = jnp.mean(yc * yc, axis=-1, keepdims=True)
    return yc * lax.rsqrt(var + LN_EPS) * g + b


def _silu(x):
    return x * jax.nn.sigmoid(x)


def _ffn_ln_kernel(x_ref, wg_ref, wu_ref, wd_ref, g_ref, b_ref, *refs, n_hidden_tiles, emit_bf16):
    if emit_bf16:
        o_ref, ob_ref, xb_ref, acc_ref = refs
    else:
        o_ref, xb_ref, acc_ref = refs
    j = pl.program_id(1)

    @pl.when(j == 0)
    def _():
        xb_ref[...] = x_ref[...].astype(BF16)
        acc_ref[...] = jnp.zeros_like(acc_ref)

    xb = xb_ref[...]
    gate = _dot(xb, wg_ref[...])
    up = _dot(xb, wu_ref[...])
    act = (_silu(gate) * up).astype(BF16)
    acc_ref[...] += _dot(act, wd_ref[...])

    @pl.when(j == n_hidden_tiles - 1)
    def _():
        y = DEEPNORM_ALPHA * x_ref[...] + 0.5 * acc_ref[...]
        out = _layer_norm(y, g_ref[...], b_ref[...])
        o_ref[...] = out
        if emit_bf16:
            ob_ref[...] = out.astype(BF16)


def _ffn_ln(x, wg, wu, wd, g, b, *, tm, th, emit_bf16, name):
    t, d = x.shape
    hidden = wg.shape[1]
    assert t % tm == 0 and hidden % th == 0
    row = pl.BlockSpec((tm, d), lambda i, j: (i, 0))
    vec = pl.BlockSpec((1, d), lambda i, j: (0, 0))
    out_shape = [jax.ShapeDtypeStruct((t, d), F32)]
    out_specs = [row]
    if emit_bf16:
        out_shape.append(jax.ShapeDtypeStruct((t, d), BF16))
        out_specs.append(row)
    block_bytes = (2 * tm * d * 4 + 2 * 3 * d * th * 2 + 2 * tm * d * 4 + (2 * tm * d * 2 if emit_bf16 else 0)
                   + tm * d * 2 + tm * d * 4)
    return pl.pallas_call(
        functools.partial(_ffn_ln_kernel, n_hidden_tiles=hidden // th, emit_bf16=emit_bf16),
        grid=(t // tm, hidden // th),
        in_specs=[row,
                  pl.BlockSpec((d, th), lambda i, j: (0, j)),
                  pl.BlockSpec((d, th), lambda i, j: (0, j)),
                  pl.BlockSpec((th, d), lambda i, j: (j, 0)),
                  vec, vec],
        out_specs=out_specs,
        out_shape=out_shape,
        scratch_shapes=[pltpu.VMEM((tm, d), BF16), pltpu.VMEM((tm, d), F32)],
        compiler_params=pltpu.CompilerParams(
            dimension_semantics=("parallel", "arbitrary"), vmem_limit_bytes=_vmem_limit(block_bytes)),
        name=name,
    )(x, wg, wu, wd, g, b)


def _matmul_kernel(a_ref, w_ref, o_ref):
    o_ref[...] = _dot(a_ref[...], w_ref[...]).astype(o_ref.dtype)


def _matmul(a, w, *, tm, tn, out_dtype, name):
    m, k = a.shape
    n = w.shape[1]
    assert m % tm == 0 and n % tn == 0
    block_bytes = 2 * (tm * k * 2 + k * tn * 2 + tm * tn * jnp.dtype(out_dtype).itemsize)
    return pl.pallas_call(
        _matmul_kernel,
        grid=(m // tm, n // tn),
        in_specs=[pl.BlockSpec((tm, k), lambda i, j: (i, 0)),
                  pl.BlockSpec((k, tn), lambda i, j: (0, j))],
        out_specs=pl.BlockSpec((tm, tn), lambda i, j: (i, j)),
        out_shape=jax.ShapeDtypeStruct((m, n), out_dtype),
        compiler_params=pltpu.CompilerParams(
            dimension_semantics=("parallel", "parallel"), vmem_limit_bytes=_vmem_limit(block_bytes)),
        name=name,
    )(a, w)


def _ssd_kernel(x_ref, b_ref, c_ref, z_ref, dtc_ref, dtr_ref,
                cwx_ref, cwb_ref, cwc_ref, cbx_ref, cbb_ref, cbc_ref,
                bias_r_ref, alog_r_ref, bias_c_ref, alog_c_ref, dskip_ref, ng_ref,
                y_ref, xpad_ref, bpad_ref, cpad_ref, state_ref):
    q = SSD_CHUNK
    tail = V7X_SUBLANES
    heads = x_ref.shape[1] // SSD_HEAD_DIM
    pairs = heads // 2
    c = pl.program_id(1)

    @pl.when(c == 0)
    def _():
        xpad_ref[0:tail, :] = jnp.zeros((tail, xpad_ref.shape[1]), F32)
        bpad_ref[0:tail, :] = jnp.zeros((tail, bpad_ref.shape[1]), F32)
        cpad_ref[0:tail, :] = jnp.zeros((tail, cpad_ref.shape[1]), F32)
        state_ref[...] = jnp.zeros_like(state_ref)

    def conv_silu(u_ref, pad_ref, w_ref, bias_ref):
        pad_ref[tail:tail + q, :] = u_ref[...]
        acc = bias_ref[...]
        for k in range(SSD_CONV_WIDTH):
            start = tail - (SSD_CONV_WIDTH - 1) + k
            acc = acc + w_ref[k:k + 1, :] * pad_ref[start:start + q, :]
        pad_ref[0:tail, :] = pad_ref[q:q + tail, :]
        return _silu(acc)

    xs = conv_silu(x_ref, xpad_ref, cwx_ref, cbx_ref)
    bm = conv_silu(b_ref, bpad_ref, cwb_ref, cbb_ref)
    cm = conv_silu(c_ref, cpad_ref, cwc_ref, cbc_ref)
    bm_b = bm.astype(BF16)
    cm_b = cm.astype(BF16)

    dt_c = jax.nn.softplus(dtc_ref[...] + bias_r_ref[...])
    dt_r = jax.nn.softplus(dtr_ref[...] + bias_c_ref[...])
    adt_c = dt_c * (-jnp.exp(alog_r_ref[...]))
    adt_r = dt_r * (-jnp.exp(alog_c_ref[...]))
    ri = lax.broadcasted_iota(jnp.int32, (q, q), 0)
    ci = lax.broadcasted_iota(jnp.int32, (q, q), 1)
    causal = ci <= ri
    cs_c = _dot_f32(causal.astype(F32), adt_c)
    cs_r = _dot_f32(adt_r, (ri <= ci).astype(F32))

    cb = _dot_nt(cm_b, bm_b)
    y_off_all = _dot(cm_b, state_ref[...].astype(BF16))

    lo = lax.broadcasted_iota(jnp.int32, (q, 2 * SSD_HEAD_DIM), 1) < SSD_HEAD_DIM
    lo_row = lax.broadcasted_iota(jnp.int32, (1, 2 * SSD_HEAD_DIM), 1) < SSD_HEAD_DIM
    y_parts, xdd_parts, cd_parts = [], [], []
    for p in range(pairs):
        h0, h1 = 2 * p, 2 * p + 1
        sl = slice(p * 2 * SSD_HEAD_DIM, (p + 1) * 2 * SSD_HEAD_DIM)
        xp = xs[:, sl]
        xd = xp * jnp.where(lo, dt_c[:, h0:h0 + 1], dt_c[:, h1:h1 + 1])
        xd_b = xd.astype(BF16)
        y_diag = []
        for h in (h0, h1):
            seg = jnp.where(causal, jnp.exp(cs_c[:, h:h + 1] - cs_r[h:h + 1, :]), 0.0)
            y_diag.append(_dot((cb * seg).astype(BF16), xd_b))
        cs_p = jnp.where(lo, cs_c[:, h0:h0 + 1], cs_c[:, h1:h1 + 1])
        cs_end = jnp.where(lo_row, cs_c[q - 1:q, h0:h0 + 1], cs_c[q - 1:q, h1:h1 + 1])
        y_p = jnp.where(lo, y_diag[0], y_diag[1]) + y_off_all[:, sl] * jnp.exp(cs_p)
        y_parts.append(y_p + dskip_ref[:, sl] * xp)
        xdd_parts.append((xd * jnp.exp(cs_end - cs_p)).astype(BF16))
        cd_parts.append(jnp.exp(cs_end))

    xdd = jnp.concatenate(xdd_parts, axis=1)
    chunk_decay = jnp.concatenate(cd_parts, axis=1)
    state_ref[...] = state_ref[...] * chunk_decay + _dot_tn(bm_b, xdd)

    y = jnp.concatenate(y_parts, axis=1) * _silu(z_ref[...])
    y = y * lax.rsqrt(jnp.mean(y * y, axis=-1, keepdims=True) + RMS_EPS) * ng_ref[...]
    y_ref[...] = y.astype(y_ref.dtype)


def _ssd(proj, dt_raw, conv_w, conv_b, dt_bias, a_log, d_skip, norm_g, *, d_inner, z_col, xbc_col):
    t = proj.shape[0]
    q = SSD_CHUNK
    g = SSD_N_GROUPS
    gw = d_inner // g
    hg = gw // SSD_HEAD_DIM
    n = SSD_D_STATE
    assert t % q == 0 and z_col % gw == 0 and xbc_col % gw == 0
    b_col = xbc_col + d_inner
    c_col = b_col + g * n
    dt_c = dt_raw.reshape(t, g, hg).transpose(1, 0, 2)
    dt_r = dt_raw.T.reshape(g, hg, t)
    conv_b2 = conv_b.reshape(1, -1)
    per_head_r = lambda v: v.reshape(g, 1, hg)
    per_head_c = lambda v: v.reshape(g, hg, 1)
    dskip_l = jnp.repeat(d_skip, SSD_HEAD_DIM).reshape(1, d_inner)
    ng = norm_g.reshape(1, d_inner)

    row_r = pl.BlockSpec((None, 1, hg), lambda gi, ci: (gi, 0, 0))
    row_c = pl.BlockSpec((None, hg, 1), lambda gi, ci: (gi, 0, 0))
    in_specs = [
        pl.BlockSpec((q, gw), lambda gi, ci: (ci, xbc_col // gw + gi)),
        pl.BlockSpec((q, n), lambda gi, ci: (ci, b_col // n + gi)),
        pl.BlockSpec((q, n), lambda gi, ci: (ci, c_col // n + gi)),
        pl.BlockSpec((q, gw), lambda gi, ci: (ci, z_col // gw + gi)),
        pl.BlockSpec((None, q, hg), lambda gi, ci: (gi, ci, 0)),
        pl.BlockSpec((None, hg, q), lambda gi, ci: (gi, 0, ci)),
        pl.BlockSpec((SSD_CONV_WIDTH, gw), lambda gi, ci: (0, gi)),
        pl.BlockSpec((SSD_CONV_WIDTH, n), lambda gi, ci: (0, d_inner // n + gi)),
        pl.BlockSpec((SSD_CONV_WIDTH, n), lambda gi, ci: (0, (d_inner + g * n) // n + gi)),
        pl.BlockSpec((1, gw), lambda gi, ci: (0, gi)),
        pl.BlockSpec((1, n), lambda gi, ci: (0, d_inner // n + gi)),
        pl.BlockSpec((1, n), lambda gi, ci: (0, (d_inner + g * n) // n + gi)),
        row_r, row_r, row_c, row_c,
        pl.BlockSpec((1, gw), lambda gi, ci: (0, gi)),
        pl.BlockSpec((1, gw), lambda gi, ci: (0, gi)),
    ]
    pad_rows = q + V7X_SUBLANES
    block_bytes = 2 * (2 * q * gw * 4 + 2 * q * n * 4 + q * gw * 2) + pad_rows * (gw + 2 * n) * 4 + n * gw * 4
    return pl.pallas_call(
        _ssd_kernel,
        grid=(g, t // q),
        in_specs=in_specs,
        out_specs=pl.BlockSpec((q, gw), lambda gi, ci: (ci, gi)),
        out_shape=jax.ShapeDtypeStruct((t, d_inner), BF16),
        scratch_shapes=[pltpu.VMEM((pad_rows, gw), F32), pltpu.VMEM((pad_rows, n), F32),
                        pltpu.VMEM((pad_rows, n), F32), pltpu.VMEM((n, gw), F32)],
        compiler_params=pltpu.CompilerParams(
            dimension_semantics=("parallel", "arbitrary"), vmem_limit_bytes=_vmem_limit(block_bytes)),
        name="ssd",
    )(proj, proj, proj, proj, dt_c, dt_r, conv_w, conv_w, conv_w, conv_b2, conv_b2, conv_b2,
      per_head_r(dt_bias), per_head_r(a_log), per_head_c(dt_bias), per_head_c(a_log), dskip_l, ng)


def _swa_kernel(q_ref, kc_ref, kp_ref, vc_ref, vp_ref, posc_ref, posp_ref, invf_ref, sink_ref, o_ref):
    w = ATTN_WINDOW
    dh = ATTN_HEAD_DIM
    half = dh // 2
    lanes = 2 * dh
    n_kv = kc_ref.shape[1] // dh
    nb = pl.program_id(0)

    lane = lax.broadcasted_iota(jnp.int32, (w, lanes), 1)
    lo = lane < dh
    first_half = (lane % dh) < half
    lane2 = lax.broadcasted_iota(jnp.int32, (2 * w, lanes), 1)
    lo2 = lane2 < dh

    def cos_sin(pos_ref):
        ang = pos_ref[...].astype(F32) * invf_ref[...]
        return jnp.cos(ang), jnp.where(first_half, -jnp.sin(ang), jnp.sin(ang))

    def rope(u, cos, sin_signed):
        partner = jnp.where(first_half, pltpu.roll(u, lanes - half, 1), pltpu.roll(u, half, 1))
        return u * cos + partner * sin_signed

    cos_c, sin_c = cos_sin(posc_ref)
    cos_p, sin_p = cos_sin(posp_ref)

    qi = lax.broadcasted_iota(jnp.int32, (w, 2 * w), 0)
    kj = lax.broadcasted_iota(jnp.int32, (w, 2 * w), 1)
    valid = (kj > qi) & (kj <= qi + w) & ((kj >= w) | (nb > 0))
    scale = dh ** -0.5

    out_tiles = [None] * (q_ref.shape[1] // lanes)
    for hk in range(n_kv):
        tile, side = divmod(hk, 2)
        sl = slice(tile * lanes, (tile + 1) * lanes)
        k_pair = jnp.concatenate([rope(kp_ref[:, sl], cos_p, sin_p), rope(kc_ref[:, sl], cos_c, sin_c)], axis=0)
        v_pair = jnp.concatenate([vp_ref[:, sl], vc_ref[:, sl]], axis=0)
        if side == 0:
            k2 = jnp.where(lo2, k_pair, pltpu.roll(k_pair, dh, 1))
            v2 = jnp.where(lo2, v_pair, pltpu.roll(v_pair, dh, 1))
        else:
            k2 = jnp.where(lo2, pltpu.roll(k_pair, dh, 1), k_pair)
            v2 = jnp.where(lo2, pltpu.roll(v_pair, dh, 1), v_pair)
        k2 = k2.astype(BF16)
        v2 = v2.astype(BF16)
        for r in range(0, ATTN_GROUP, 2):
            qt = (hk * ATTN_GROUP + r) // 2
            q_pair = rope(q_ref[:, qt * lanes:(qt + 1) * lanes], cos_c, sin_c) * scale
            o_heads = []
            for hq in range(2):
                head = hk * ATTN_GROUP + r + hq
                q_one = jnp.where(lo if hq == 0 else ~lo, q_pair, 0.0).astype(BF16)
                s = jnp.where(valid, _dot_nt(q_one, k2), -jnp.inf)
                sink = sink_ref[0:1, head:head + 1]
                m = jnp.maximum(jnp.max(s, axis=-1, keepdims=True), sink)
                p = jnp.exp(s - m)
                denom = jnp.sum(p, axis=-1, keepdims=True) + jnp.exp(sink - m)
                o_heads.append(_dot(p.astype(BF16), v2) / denom)
            out_tiles[qt] = jnp.where(lo, o_heads[0], o_heads[1])
    o_ref[...] = jnp.concatenate(out_tiles, axis=1).astype(o_ref.dtype)


def _swa(proj, positions, sinks, *, q_col, q_width, k_col, v_col, kv_width):
    t = proj.shape[0]
    w = ATTN_WINDOW
    assert t % w == 0 and q_col % q_width == 0 and k_col % kv_width == 0 and v_col % kv_width == 0
    half = ATTN_HEAD_DIM // 2
    inv_freq = ROPE_THETA ** (-jnp.arange(half, dtype=F32) * 2.0 / ATTN_HEAD_DIM)
    inv_freq = jnp.tile(inv_freq, 2 * V7X_LANES // ATTN_HEAD_DIM).reshape(1, V7X_LANES)
    pos = positions.reshape(t, 1)
    prev = lambda i: jnp.maximum(i - 1, 0)
    block_bytes = 2 * (w * q_width * 4 + 4 * w * kv_width * 4 + w * q_width * 2)
    return pl.pallas_call(
        _swa_kernel,
        grid=(t // w,),
        in_specs=[
            pl.BlockSpec((w, q_width), lambda i: (i, q_col // q_width)),
            pl.BlockSpec((w, kv_width), lambda i: (i, k_col // kv_width)),
            pl.BlockSpec((w, kv_width), lambda i: (prev(i), k_col // kv_width)),
            pl.BlockSpec((w, kv_width), lambda i: (i, v_col // kv_width)),
            pl.BlockSpec((w, kv_width), lambda i: (prev(i), v_col // kv_width)),
            pl.BlockSpec((w, 1), lambda i: (i, 0)),
            pl.BlockSpec((w, 1), lambda i: (prev(i), 0)),
            pl.BlockSpec((1, V7X_LANES), lambda i: (0, 0)),
            pl.BlockSpec((1, sinks.shape[0]), lambda i: (0, 0)),
        ],
        out_specs=pl.BlockSpec((w, q_width), lambda i: (i, 0)),
        out_shape=jax.ShapeDtypeStruct((t, q_width), BF16),
        compiler_params=pltpu.CompilerParams(
            dimension_semantics=("parallel",), vmem_limit_bytes=_vmem_limit(block_bytes)),
        name="swa",
    )(proj, proj, proj, proj, proj, pos, pos, inv_freq, sinks.reshape(1, -1))


def _merge_kernel(ys_ref, ya_ref, ws_ref, wa_ref, gs_ref, ga_ref, o_ref):
    y_s = _dot(ys_ref[...], ws_ref[...])
    y_a = _dot(ya_ref[...], wa_ref[...])
    merged = jax.nn.sigmoid(gs_ref[...]) * y_s + jax.nn.sigmoid(ga_ref[...]) * y_a
    o_ref[...] = merged.astype(o_ref.dtype)


def _merge(y_ssd, y_attn, w_ssd_o, w_attn_o, proj, *, gs_col, ga_col, tm, tn):
    t, ks = y_ssd.shape
    ka = y_attn.shape[1]
    d = w_ssd_o.shape[1]
    assert t % tm == 0 and d % tn == 0 and gs_col % tn == 0 and ga_col % tn == 0
    block_bytes = 2 * (tm * ks * 2 + tm * ka * 2 + ks * tn * 2 + ka * tn * 2 + 2 * tm * tn * 4 + tm * tn * 2)
    return pl.pallas_call(
        _merge_kernel,
        grid=(t // tm, d // tn),
        in_specs=[pl.BlockSpec((tm, ks), lambda i, j: (i, 0)),
                  pl.BlockSpec((tm, ka), lambda i, j: (i, 0)),
                  pl.BlockSpec((ks, tn), lambda i, j: (0, j)),
                  pl.BlockSpec((ka, tn), lambda i, j: (0, j)),
                  pl.BlockSpec((tm, tn), lambda i, j: (i, gs_col // tn + j)),
                  pl.BlockSpec((tm, tn), lambda i, j: (i, ga_col // tn + j))],
        out_specs=pl.BlockSpec((tm, tn), lambda i, j: (i, j)),
        out_shape=jax.ShapeDtypeStruct((t, d), BF16),
        compiler_params=pltpu.CompilerParams(
            dimension_semantics=("parallel", "parallel"), vmem_limit_bytes=_vmem_limit(block_bytes)),
        name="merge",
    )(y_ssd, y_attn, w_ssd_o, w_attn_o, proj, proj)


def _proj_ln_kernel(m_ref, w_ref, h_ref, g_ref, b_ref, o_ref):
    y = DEEPNORM_ALPHA * h_ref[...] + _dot(m_ref[...], w_ref[...])
    o_ref[...] = _layer_norm(y, g_ref[...], b_ref[...])


def _proj_ln(merged, w_out, h, g, b, *, tm):
    t, d = h.shape
    k = merged.shape[1]
    assert t % tm == 0
    vec = pl.BlockSpec((1, d), lambda i: (0, 0))
    block_bytes = 2 * (tm * k * 2 + k * d * 2 + 2 * tm * d * 4)
    return pl.pallas_call(
        _proj_ln_kernel,
        grid=(t // tm,),
        in_specs=[pl.BlockSpec((tm, k), lambda i: (i, 0)),
                  pl.BlockSpec((k, d), lambda i: (0, 0)),
                  pl.BlockSpec((tm, d), lambda i: (i, 0)),
                  vec, vec],
        out_specs=pl.BlockSpec((tm, d), lambda i: (i, 0)),
        out_shape=jax.ShapeDtypeStruct((t, d), F32),
        compiler_params=pltpu.CompilerParams(
            dimension_semantics=("parallel",), vmem_limit_bytes=_vmem_limit(block_bytes)),
        name="proj_ln",
    )(merged, w_out, h, g, b)


def _layer(x, positions, ffn1_w_gate, ffn1_w_up, ffn1_w_down, ln1_g, ln1_b,
           w_in, conv_w, conv_b, dt_bias, a_log, d_skip, ssd_norm_g, w_ssd_o,
           attn_sinks, w_attn_o, w_out, ln2_g, ln2_b,
           ffn2_w_gate, ffn2_w_up, ffn2_w_down, ln3_g, ln3_b):
    t, d = x.shape
    d_inner = w_ssd_o.shape[0]
    n_ssd_heads = dt_bias.shape[0]
    q_width = w_attn_o.shape[0]
    kv_width = q_width // ATTN_GROUP
    xbc_width = d_inner + 2 * SSD_N_GROUPS * SSD_D_STATE
    vec = lambda v: v.reshape(1, -1)
    tm_ffn = min(512, t)
    tm_mm = min(1024, t)

    sizes = (d_inner, xbc_width, n_ssd_heads, q_width, kv_width, kv_width, d, d)
    starts = [0]
    for s in sizes:
        starts.append(starts[-1] + s)
    dt_lo, dt_hi = starts[2], starts[3]
    w_main = jnp.concatenate([w_in[:, :dt_lo], w_in[:, dt_hi:]], axis=1).astype(BF16)
    w_dt = w_in[:, dt_lo:dt_hi].astype(BF16)
    col = {"z": 0, "xbc": d_inner, "q": dt_lo, "k": dt_lo + q_width, "v": dt_lo + q_width + kv_width,
           "gs": dt_lo + q_width + 2 * kv_width, "ga": dt_lo + q_width + 2 * kv_width + d}

    h1, h1b = _ffn_ln(x, ffn1_w_gate.astype(BF16), ffn1_w_up.astype(BF16), ffn1_w_down.astype(BF16),
                      vec(ln1_g), vec(ln1_b), tm=tm_ffn, th=512, emit_bf16=True, name="ffn1_ln")

    proj = _matmul(h1b, w_main, tm=tm_mm, tn=768, out_dtype=F32, name="in_proj")
    dt_raw = _matmul(h1b, w_dt, tm=tm_mm, tn=n_ssd_heads, out_dtype=F32, name="dt_proj")

    y_ssd = _ssd(proj, dt_raw, conv_w, conv_b, dt_bias, a_log, d_skip, ssd_norm_g,
                 d_inner=d_inner, z_col=col["z"], xbc_col=col["xbc"])
    y_attn = _swa(proj, positions, attn_sinks, q_col=col["q"], q_width=q_width,
                  k_col=col["k"], v_col=col["v"], kv_width=kv_width)

    merged = _merge(y_ssd, y_attn, w_ssd_o.astype(BF16), w_attn_o.astype(BF16), proj,
                    gs_col=col["gs"], ga_col=col["ga"], tm=tm_ffn, tn=512)
    h2 = _proj_ln(merged, w_out.astype(BF16), h1, vec(ln2_g), vec(ln2_b), tm=tm_ffn)

    (out,) = _ffn_ln(h2, ffn2_w_gate.astype(BF16), ffn2_w_up.astype(BF16), ffn2_w_down.astype(BF16),
                     vec(ln3_g), vec(ln3_b), tm=tm_ffn, th=512, emit_bf16=False, name="ffn2_ln")
    return out


def kernel(x, positions, ffn1_w_gate, ffn1_w_up, ffn1_w_down, ln1_g, ln1_b, w_in, conv_w, conv_b, dt_bias, a_log, d_skip, ssd_norm_g, w_ssd_o, attn_sinks, w_attn_o, w_out, ln2_g, ln2_b, ffn2_w_gate, ffn2_w_up, ffn2_w_down, ln3_g, ln3_b):
    batch, depth = x.shape[0], ffn1_w_gate.shape[0]
    assert depth == DEPTH
    outs = []
    for bi in range(batch):
        h = x[bi]
        for l in range(depth):
            h = _layer(h, positions[bi], ffn1_w_gate[l], ffn1_w_up[l], ffn1_w_down[l], ln1_g[l], ln1_b[l],
                       w_in[l], conv_w[l], conv_b[l], dt_bias[l], a_log[l], d_skip[l], ssd_norm_g[l], w_ssd_o[l],
                       attn_sinks[l], w_attn_o[l], w_out[l], ln2_g[l], ln2_b[l],
                       ffn2_w_gate[l], ffn2_w_up[l], ffn2_w_down[l], ln3_g[l], ln3_b[l])
        outs.append(h)
    return jnp.stack(outs, axis=0)
```

```python
import functools

import jax
import jax.numpy as jnp
from jax import lax
from jax.experimental import pallas as pl
from jax.experimental.pallas import tpu as pltpu

F32 = jnp.float32
BF16 = jnp.bfloat16

V7X_LANES = 128
V7X_SUBLANES = 8
V7X_VMEM_BYTES = 64 * 1024 * 1024

SSD_HEAD_DIM = 64
SSD_N_GROUPS = 8
SSD_D_STATE = 128
SSD_CONV_WIDTH = 4
SSD_CHUNK = 128
ATTN_HEAD_DIM = 64
ATTN_GROUP = 8
ATTN_WINDOW = 128
ROPE_THETA = 10000.0
DEPTH = 1
DEEPNORM_ALPHA = (2 * DEPTH) ** 0.25
LN_EPS = 1e-5
RMS_EPS = 1e-5

CAST_ROWS = 256


def _vmem_limit(block_bytes):
    return int(min(block_bytes * 5 // 4 + (8 << 20), V7X_VMEM_BYTES - (6 << 20)))


def _dot(a, b):
    return jnp.dot(a, b, preferred_element_type=F32)


def _dot_nt(a, b):
    return lax.dot_general(a, b, (((1,), (1,)), ((), ())), preferred_element_type=F32)


def _dot_tn(a, b):
    return lax.dot_general(a, b, (((0,), (0,)), ((), ())), preferred_element_type=F32)


def _dot_f32(a, b):
    return jnp.dot(a, b, preferred_element_type=F32, precision=lax.Precision.HIGHEST)


def _layer_norm(y, g, b):
    mu = jnp.mean(y, axis=-1, keepdims=True)
    yc = y - mu
    var = jnp.mean(yc * yc, axis=-1, keepdims=True)
    return yc * lax.rsqrt(var + LN_EPS) * g + b


def _silu(x):
    return x * jax.nn.sigmoid(x)


def _cast_weight(dst_ref, w_ref):
    k = dst_ref.shape[0]
    for r0 in range(0, k, CAST_ROWS):
        rows = slice(r0, min(r0 + CAST_ROWS, k))
        dst_ref[rows, :] = w_ref[rows, :].astype(BF16)


def _ffn_ln_kernel(x_ref, wg_ref, wu_ref, wd_ref, g_ref, b_ref, *refs, n_hidden_tiles, emit_bf16):
    if emit_bf16:
        o_ref, ob_ref, xb_ref, acc_ref = refs
    else:
        o_ref, xb_ref, acc_ref = refs
    j = pl.program_id(1)

    @pl.when(j == 0)
    def _():
        xb_ref[...] = x_ref[...].astype(BF16)
        acc_ref[...] = jnp.zeros_like(acc_ref)

    xb = xb_ref[...]
    gate = _dot(xb, wg_ref[...])
    up = _dot(xb, wu_ref[...])
    act = (_silu(gate) * up).astype(BF16)
    acc_ref[...] += _dot(act, wd_ref[...])

    @pl.when(j == n_hidden_tiles - 1)
    def _():
        y = DEEPNORM_ALPHA * x_ref[...] + 0.5 * acc_ref[...]
        out = _layer_norm(y, g_ref[...], b_ref[...])
        o_ref[...] = out
        if emit_bf16:
            ob_ref[...] = out.astype(BF16)


def _ffn_ln(x, wg, wu, wd, g, b, *, tm, th, emit_bf16, name):
    t, d = x.shape
    hidden = wg.shape[1]
    assert t % tm == 0 and hidden % th == 0
    row = pl.BlockSpec((tm, d), lambda i, j: (i, 0))
    vec = pl.BlockSpec((1, d), lambda i, j: (0, 0))
    out_shape = [jax.ShapeDtypeStruct((t, d), F32)]
    out_specs = [row]
    if emit_bf16:
        out_shape.append(jax.ShapeDtypeStruct((t, d), BF16))
        out_specs.append(row)
    block_bytes = (2 * tm * d * 4 + 2 * 3 * d * th * 2 + 2 * tm * d * 4 + (2 * tm * d * 2 if emit_bf16 else 0)
                   + tm * d * 2 + tm * d * 4)
    return pl.pallas_call(
        functools.partial(_ffn_ln_kernel, n_hidden_tiles=hidden // th, emit_bf16=emit_bf16),
        grid=(t // tm, hidden // th),
        in_specs=[row,
                  pl.BlockSpec((d, th), lambda i, j: (0, j)),
                  pl.BlockSpec((d, th), lambda i, j: (0, j)),
                  pl.BlockSpec((th, d), lambda i, j: (j, 0)),
                  vec, vec],
        out_specs=out_specs,
        out_shape=out_shape,
        scratch_shapes=[pltpu.VMEM((tm, d), BF16), pltpu.VMEM((tm, d), F32)],
        compiler_params=pltpu.CompilerParams(
            dimension_semantics=("parallel", "arbitrary"), vmem_limit_bytes=_vmem_limit(block_bytes)),
        name=name,
    )(x, wg, wu, wd, g, b)


def _in_proj_kernel(a_ref, w_ref, o_ref, wb_ref):
    @pl.when(pl.program_id(1) == 0)
    def _():
        _cast_weight(wb_ref, w_ref)

    o_ref[...] = _dot_nt(a_ref[...], wb_ref[...])


def _in_proj(a, w_t, *, row0, n, tm, tn, name):
    m, k = a.shape
    assert m % tm == 0 and n % tn == 0 and row0 % V7X_SUBLANES == 0
    w_spec = pl.BlockSpec((pl.Element(tn), pl.Element(k)),
                          lambda j, i: (pl.multiple_of(row0 + j * tn, V7X_SUBLANES), 0))
    block_bytes = 2 * (tm * k * 2 + tn * k * 4 + tm * tn * 4) + tn * k * 2
    return pl.pallas_call(
        _in_proj_kernel,
        grid=(n // tn, m // tm),
        in_specs=[pl.BlockSpec((tm, k), lambda j, i: (i, 0)), w_spec],
        out_specs=pl.BlockSpec((tm, tn), lambda j, i: (i, j)),
        out_shape=jax.ShapeDtypeStruct((m, n), F32),
        scratch_shapes=[pltpu.VMEM((tn, k), BF16)],
        compiler_params=pltpu.CompilerParams(
            dimension_semantics=("parallel", "arbitrary"), vmem_limit_bytes=_vmem_limit(block_bytes)),
        name=name,
    )(a, w_t)


def _dt_proj_kernel(a_ref, w_ref, o_ref):
    o_ref[...] = _dot_nt(w_ref[...].astype(BF16), a_ref[...])


def _dt_proj(a, w_t, *, row0, n_heads, tm):
    m, k = a.shape
    assert row0 % n_heads == 0 and n_heads % V7X_SUBLANES == 0 and m % tm == 0
    block_bytes = 2 * (tm * k * 2 + n_heads * k * 4 + n_heads * tm * 4)
    return pl.pallas_call(
        _dt_proj_kernel,
        grid=(m // tm,),
        in_specs=[pl.BlockSpec((tm, k), lambda i: (i, 0)),
                  pl.BlockSpec((n_heads, k), lambda i: (row0 // n_heads, 0))],
        out_specs=pl.BlockSpec((n_heads, tm), lambda i: (0, i)),
        out_shape=jax.ShapeDtypeStruct((n_heads, m), F32),
        compiler_params=pltpu.CompilerParams(
            dimension_semantics=("parallel",), vmem_limit_bytes=_vmem_limit(block_bytes)),
        name="dt_proj",
    )(a, w_t)


def _ssd_kernel(x_ref, b_ref, c_ref, z_ref, dt_ref,
                cwx_ref, cwb_ref, cwc_ref, cbx_ref, cbb_ref, cbc_ref,
                bias_ref, alog_ref, dskip_ref, ng_ref,
                y_ref, xpad_ref, bpad_ref, cpad_ref, xs_ref, bm_ref, cm_ref, state_ref):
    q = SSD_CHUNK
    tail = V7X_SUBLANES
    rows = x_ref.shape[0]
    heads = x_ref.shape[1] // SSD_HEAD_DIM
    pairs = heads // 2
    pw = 2 * SSD_HEAD_DIM

    @pl.when(pl.program_id(1) == 0)
    def _():
        xpad_ref[0:tail, :] = jnp.zeros((tail, xpad_ref.shape[1]), F32)
        bpad_ref[0:tail, :] = jnp.zeros((tail, bpad_ref.shape[1]), F32)
        cpad_ref[0:tail, :] = jnp.zeros((tail, cpad_ref.shape[1]), F32)
        state_ref[...] = jnp.zeros_like(state_ref)

    def conv_silu(u_ref, pad_ref, w_ref, bias_ref, dst_ref):
        pad_ref[tail:tail + rows, :] = u_ref[...]
        acc = bias_ref[...]
        for k in range(SSD_CONV_WIDTH):
            start = tail - (SSD_CONV_WIDTH - 1) + k
            acc = acc + w_ref[k:k + 1, :] * pad_ref[start:start + rows, :]
        pad_ref[0:tail, :] = pad_ref[rows:rows + tail, :]
        dst_ref[...] = _silu(acc).astype(dst_ref.dtype)

    conv_silu(x_ref, xpad_ref, cwx_ref, cbx_ref, xs_ref)
    conv_silu(b_ref, bpad_ref, cwb_ref, cbb_ref, bm_ref)
    conv_silu(c_ref, cpad_ref, cwc_ref, cbc_ref, cm_ref)

    dt_all = jax.nn.softplus(dt_ref[...] + bias_ref[...])
    adt_all = dt_all * (-jnp.exp(alog_ref[...]))

    ri = lax.broadcasted_iota(jnp.int32, (q, q), 0)
    ci = lax.broadcasted_iota(jnp.int32, (q, q), 1)
    causal = ci <= ri
    upper = (ri <= ci).astype(F32)
    lo = lax.broadcasted_iota(jnp.int32, (q, pw), 1) < SSD_HEAD_DIM
    lo_row = lax.broadcasted_iota(jnp.int32, (1, pw), 1) < SSD_HEAD_DIM

    for ck in range(rows // q):
        r = slice(ck * q, (ck + 1) * q)
        dt_r = dt_all[:, r]
        cs_r = _dot_f32(adt_all[:, r], upper)
        both = jnp.concatenate([dt_r, cs_r, jnp.zeros((q - 2 * heads, q), F32)], axis=0).T
        dt_c = both[:, 0:heads]
        cs_c = both[:, heads:2 * heads]

        bm_b = bm_ref[r, :]
        cm_b = cm_ref[r, :]
        cb = _dot_nt(cm_b, bm_b)
        y_off_all = _dot(cm_b, state_ref[...].astype(BF16))

        y_parts, xdd_parts, cd_parts = [], [], []
        for p in range(pairs):
            h0, h1 = 2 * p, 2 * p + 1
            sl = slice(p * pw, (p + 1) * pw)
            xp = xs_ref[r, sl]
            xd = xp * jnp.where(lo, dt_c[:, h0:h0 + 1], dt_c[:, h1:h1 + 1])
            xd_b = xd.astype(BF16)
            y_diag = []
            for h in (h0, h1):
                seg = jnp.where(causal, jnp.exp(cs_c[:, h:h + 1] - cs_r[h:h + 1, :]), 0.0)
                y_diag.append(_dot((cb * seg).astype(BF16), xd_b))
            cs_p = jnp.where(lo, cs_c[:, h0:h0 + 1], cs_c[:, h1:h1 + 1])
            cs_end = jnp.where(lo_row, cs_c[q - 1:q, h0:h0 + 1], cs_c[q - 1:q, h1:h1 + 1])
            y_p = jnp.where(lo, y_diag[0], y_diag[1]) + y_off_all[:, sl] * jnp.exp(cs_p)
            y_parts.append(y_p + dskip_ref[:, sl] * xp)
            xdd_parts.append((xd * jnp.exp(cs_end - cs_p)).astype(BF16))
            cd_parts.append(jnp.exp(cs_end))

        xdd = jnp.concatenate(xdd_parts, axis=1)
        chunk_decay = jnp.concatenate(cd_parts, axis=1)
        state_ref[...] = state_ref[...] * chunk_decay + _dot_tn(bm_b, xdd)

        y = jnp.concatenate(y_parts, axis=1) * _silu(z_ref[r, :])
        y = y * lax.rsqrt(jnp.mean(y * y, axis=-1, keepdims=True) + RMS_EPS) * ng_ref[...]
        y_ref[r, :] = y.astype(y_ref.dtype)


def _ssd(proj, dt_t, conv_w, conv_b, dt_bias, a_log, d_skip, norm_g, *, d_inner, z_col, xbc_col, rows):
    t = proj.shape[0]
    g = SSD_N_GROUPS
    gw = d_inner // g
    hg = gw // SSD_HEAD_DIM
    n = SSD_D_STATE
    assert t % rows == 0 and rows % SSD_CHUNK == 0 and z_col % gw == 0 and xbc_col % gw == 0
    assert 2 * hg <= SSD_CHUNK and hg % V7X_SUBLANES == 0
    b_col = xbc_col + d_inner
    c_col = b_col + g * n
    conv_b2 = conv_b.reshape(1, -1)
    per_head = lambda v: v.reshape(g * hg, 1)
    dskip_l = jnp.repeat(d_skip, SSD_HEAD_DIM).reshape(1, d_inner)
    ng = norm_g.reshape(1, d_inner)

    head_col = pl.BlockSpec((hg, 1), lambda gi, ci: (gi, 0))
    in_specs = [
        pl.BlockSpec((rows, gw), lambda gi, ci: (ci, xbc_col // gw + gi)),
        pl.BlockSpec((rows, n), lambda gi, ci: (ci, b_col // n + gi)),
        pl.BlockSpec((rows, n), lambda gi, ci: (ci, c_col // n + gi)),
        pl.BlockSpec((rows, gw), lambda gi, ci: (ci, z_col // gw + gi)),
        pl.BlockSpec((hg, rows), lambda gi, ci: (gi, ci)),
        pl.BlockSpec((SSD_CONV_WIDTH, gw), lambda gi, ci: (0, gi)),
        pl.BlockSpec((SSD_CONV_WIDTH, n), lambda gi, ci: (0, d_inner // n + gi)),
        pl.BlockSpec((SSD_CONV_WIDTH, n), lambda gi, ci: (0, (d_inner + g * n) // n + gi)),
        pl.BlockSpec((1, gw), lambda gi, ci: (0, gi)),
        pl.BlockSpec((1, n), lambda gi, ci: (0, d_inner // n + gi)),
        pl.BlockSpec((1, n), lambda gi, ci: (0, (d_inner + g * n) // n + gi)),
        head_col, head_col,
        pl.BlockSpec((1, gw), lambda gi, ci: (0, gi)),
        pl.BlockSpec((1, gw), lambda gi, ci: (0, gi)),
    ]
    pad_rows = rows + V7X_SUBLANES
    block_bytes = (2 * (2 * rows * gw * 4 + 2 * rows * n * 4 + rows * gw * 2) + pad_rows * (gw + 2 * n) * 4
                   + rows * gw * 4 + 2 * rows * n * 2 + n * gw * 4)
    return pl.pallas_call(
        _ssd_kernel,
        grid=(g, t // rows),
        in_specs=in_specs,
        out_specs=pl.BlockSpec((rows, gw), lambda gi, ci: (ci, gi)),
        out_shape=jax.ShapeDtypeStruct((t, d_inner), BF16),
        scratch_shapes=[pltpu.VMEM((pad_rows, gw), F32), pltpu.VMEM((pad_rows, n), F32),
                        pltpu.VMEM((pad_rows, n), F32), pltpu.VMEM((rows, gw), F32),
                        pltpu.VMEM((rows, n), BF16), pltpu.VMEM((rows, n), BF16),
                        pltpu.VMEM((n, gw), F32)],
        compiler_params=pltpu.CompilerParams(
            dimension_semantics=("parallel", "arbitrary"), vmem_limit_bytes=_vmem_limit(block_bytes)),
        name="ssd",
    )(proj, proj, proj, proj, dt_t, conv_w, conv_w, conv_w, conv_b2, conv_b2, conv_b2,
      per_head(dt_bias), per_head(a_log), dskip_l, ng)


def _swa_kernel(q_ref, k_ref, v_ref, pos_ref, invf_ref, sink_ref, o_ref, kprev_ref, vprev_ref):
    w = ATTN_WINDOW
    dh = ATTN_HEAD_DIM
    half = dh // 2
    lanes = 2 * dh
    n_kv = k_ref.shape[1] // dh
    tiles_per_kv = ATTN_GROUP // 2
    nb = pl.program_id(0)

    @pl.when(nb == 0)
    def _():
        kprev_ref[...] = jnp.zeros_like(kprev_ref)
        vprev_ref[...] = jnp.zeros_like(vprev_ref)

    lane = lax.broadcasted_iota(jnp.int32, (w, lanes), 1)
    first_half = (lane % dh) < half
    lo2 = lax.broadcasted_iota(jnp.int32, (2 * w, lanes), 1) < dh

    ang = pos_ref[...].astype(F32) * invf_ref[...]
    cos = jnp.cos(ang)
    sin = jnp.sin(ang)
    sin = jnp.where(first_half, -sin, sin)
    scale = dh ** -0.5

    def rope(u, c, s):
        partner = jnp.where(first_half, pltpu.roll(u, lanes - half, 1), pltpu.roll(u, half, 1))
        return u * c + partner * s

    kj = lax.broadcasted_iota(jnp.int32, (2 * w, w), 0)
    qi = lax.broadcasted_iota(jnp.int32, (2 * w, w), 1)
    valid = (kj > qi) & (kj <= qi + w) & ((kj >= w) | (nb > 0))
    bias = jnp.where(valid, 0.0, -jnp.inf)

    cos_q, sin_q = cos * scale, sin * scale
    q_tiles = [rope(q_ref[:, t * lanes:(t + 1) * lanes], cos_q, sin_q).astype(BF16)
               for t in range(q_ref.shape[1] // lanes)]

    for pt in range(n_kv // 2):
        sl = slice(pt * lanes, (pt + 1) * lanes)
        k_cur = rope(k_ref[:, sl], cos, sin)
        v_cur = v_ref[:, sl]
        k_pair = jnp.concatenate([kprev_ref[:, sl], k_cur], axis=0)
        v_pair = jnp.concatenate([vprev_ref[:, sl], v_cur], axis=0)
        kprev_ref[:, sl] = k_cur
        vprev_ref[:, sl] = v_cur
        k_swap = pltpu.roll(k_pair, dh, 1)
        vt_pair = v_pair.T
        for side in range(2):
            hk = 2 * pt + side
            k_lo = jnp.where(lo2, k_pair if side == 0 else k_swap, 0.0).astype(BF16)
            k_hi = jnp.where(lo2, 0.0, k_swap if side == 0 else k_pair).astype(BF16)
            vt = vt_pair[side * dh:(side + 1) * dh, :].astype(BF16)
            qs = jnp.concatenate(q_tiles[hk * tiles_per_kv:(hk + 1) * tiles_per_kv], axis=0)
            o_parity = []
            for parity, k_sel in enumerate((k_lo, k_hi)):
                st = _dot_nt(k_sel, qs)
                p_parts, den_parts = [], []
                for j in range(tiles_per_kv):
                    head = hk * ATTN_GROUP + 2 * j + parity
                    s = st[:, j * w:(j + 1) * w] + bias
                    sink = sink_ref[0:1, head:head + 1]
                    m = jnp.maximum(jnp.max(s, axis=0, keepdims=True), sink)
                    p = jnp.exp(s - m)
                    den_parts.append(jnp.sum(p, axis=0, keepdims=True) + jnp.exp(sink - m))
                    p_parts.append(p.astype(BF16))
                ot = _dot(vt, jnp.concatenate(p_parts, axis=1))
                o_parity.append(ot / jnp.concatenate(den_parts, axis=1))
            for j in range(tiles_per_kv):
                qt = hk * tiles_per_kv + j
                tile_t = jnp.concatenate([o_parity[0][:, j * w:(j + 1) * w],
                                          o_parity[1][:, j * w:(j + 1) * w]], axis=0)
                o_ref[:, qt * lanes:(qt + 1) * lanes] = tile_t.T.astype(o_ref.dtype)


def _swa(proj, positions, sinks, *, q_col, q_width, k_col, v_col, kv_width):
    t = proj.shape[0]
    w = ATTN_WINDOW
    assert t % w == 0 and q_col % q_width == 0 and k_col % kv_width == 0 and v_col % kv_width == 0
    half = ATTN_HEAD_DIM // 2
    inv_freq = ROPE_THETA ** (-jnp.arange(half, dtype=F32) * 2.0 / ATTN_HEAD_DIM)
    inv_freq = jnp.tile(inv_freq, 2 * V7X_LANES // ATTN_HEAD_DIM).reshape(1, V7X_LANES)
    block_bytes = 2 * (w * q_width * 4 + 2 * w * kv_width * 4 + w * q_width * 2) + 2 * w * kv_width * 4
    return pl.pallas_call(
        _swa_kernel,
        grid=(t // w,),
        in_specs=[
            pl.BlockSpec((w, q_width), lambda i: (i, q_col // q_width)),
            pl.BlockSpec((w, kv_width), lambda i: (i, k_col // kv_width)),
            pl.BlockSpec((w, kv_width), lambda i: (i, v_col // kv_width)),
            pl.BlockSpec((w, 1), lambda i: (i, 0)),
            pl.BlockSpec((1, V7X_LANES), lambda i: (0, 0)),
            pl.BlockSpec((1, sinks.shape[0]), lambda i: (0, 0)),
        ],
        out_specs=pl.BlockSpec((w, q_width), lambda i: (i, 0)),
        out_shape=jax.ShapeDtypeStruct((t, q_width), BF16),
        scratch_shapes=[pltpu.VMEM((w, kv_width), F32), pltpu.VMEM((w, kv_width), F32)],
        compiler_params=pltpu.CompilerParams(
            dimension_semantics=("arbitrary",), vmem_limit_bytes=_vmem_limit(block_bytes)),
        name="swa",
    )(proj, proj, proj, positions.reshape(t, 1), inv_freq, sinks.reshape(1, -1))


def _merge_kernel(ys_ref, ya_ref, ws_ref, wa_ref, gs_ref, ga_ref, o_ref, wsb_ref, wab_ref):
    @pl.when(pl.program_id(1) == 0)
    def _():
        _cast_weight(wsb_ref, ws_ref)
        _cast_weight(wab_ref, wa_ref)

    y_s = _dot(ys_ref[...], wsb_ref[...])
    y_a = _dot(ya_ref[...], wab_ref[...])
    merged = jax.nn.sigmoid(gs_ref[...]) * y_s + jax.nn.sigmoid(ga_ref[...]) * y_a
    o_ref[...] = merged.astype(o_ref.dtype)


def _merge(y_ssd, y_attn, w_ssd_o, w_attn_o, proj, *, gs_col, ga_col, tm, tn):
    t, ks = y_ssd.shape
    ka = y_attn.shape[1]
    d = w_ssd_o.shape[1]
    assert t % tm == 0 and d % tn == 0 and gs_col % tn == 0 and ga_col % tn == 0
    block_bytes = (2 * (tm * ks * 2 + tm * ka * 2 + ks * tn * 4 + ka * tn * 4 + 2 * tm * tn * 4 + tm * tn * 2)
                   + (ks + ka) * tn * 2)
    return pl.pallas_call(
        _merge_kernel,
        grid=(d // tn, t // tm),
        in_specs=[pl.BlockSpec((tm, ks), lambda j, i: (i, 0)),
                  pl.BlockSpec((tm, ka), lambda j, i: (i, 0)),
                  pl.BlockSpec((ks, tn), lambda j, i: (0, j)),
                  pl.BlockSpec((ka, tn), lambda j, i: (0, j)),
                  pl.BlockSpec((tm, tn), lambda j, i: (i, gs_col // tn + j)),
                  pl.BlockSpec((tm, tn), lambda j, i: (i, ga_col // tn + j))],
        out_specs=pl.BlockSpec((tm, tn), lambda j, i: (i, j)),
        out_shape=jax.ShapeDtypeStruct((t, d), BF16),
        scratch_shapes=[pltpu.VMEM((ks, tn), BF16), pltpu.VMEM((ka, tn), BF16)],
        compiler_params=pltpu.CompilerParams(
            dimension_semantics=("parallel", "arbitrary"), vmem_limit_bytes=_vmem_limit(block_bytes)),
        name="merge",
    )(y_ssd, y_attn, w_ssd_o, w_attn_o, proj, proj)


def _proj_ln_kernel(m_ref, w_ref, h_ref, g_ref, b_ref, o_ref, wb_ref):
    @pl.when(pl.program_id(0) == 0)
    def _():
        _cast_weight(wb_ref, w_ref)

    y = DEEPNORM_ALPHA * h_ref[...] + _dot(m_ref[...], wb_ref[...])
    o_ref[...] = _layer_norm(y, g_ref[...], b_ref[...])


def _proj_ln(merged, w_out, h, g, b, *, tm):
    t, d = h.shape
    k = merged.shape[1]
    assert t % tm == 0
    vec = pl.BlockSpec((1, d), lambda i: (0, 0))
    block_bytes = 2 * (tm * k * 2 + 2 * tm * d * 4) + k * d * 4 + k * d * 2
    return pl.pallas_call(
        _proj_ln_kernel,
        grid=(t // tm,),
        in_specs=[pl.BlockSpec((tm, k), lambda i: (i, 0)),
                  pl.BlockSpec((k, d), lambda i: (0, 0), pipeline_mode=pl.Buffered(1)),
                  pl.BlockSpec((tm, d), lambda i: (i, 0)),
                  vec, vec],
        out_specs=pl.BlockSpec((tm, d), lambda i: (i, 0)),
        out_shape=jax.ShapeDtypeStruct((t, d), F32),
        scratch_shapes=[pltpu.VMEM((k, d), BF16)],
        compiler_params=pltpu.CompilerParams(
            dimension_semantics=("arbitrary",), vmem_limit_bytes=_vmem_limit(block_bytes)),
        name="proj_ln",
    )(merged, w_out, h, g, b)


def _layer(x, positions, ffn1_w_gate, ffn1_w_up, ffn1_w_down, ln1_g, ln1_b,
           w_in, conv_w, conv_b, dt_bias, a_log, d_skip, ssd_norm_g, w_ssd_o,
           attn_sinks, w_attn_o, w_out, ln2_g, ln2_b,
           ffn2_w_gate, ffn2_w_up, ffn2_w_down, ln3_g, ln3_b):
    t, d = x.shape
    d_inner = w_ssd_o.shape[0]
    n_ssd_heads = dt_bias.shape[0]
    q_width = w_attn_o.shape[0]
    kv_width = q_width // ATTN_GROUP
    xbc_width = d_inner + 2 * SSD_N_GROUPS * SSD_D_STATE
    vec = lambda v: v.reshape(1, -1)
    tm_ffn = min(512, t)
    tm_mm = min(1024, t)

    sizes = (d_inner, xbc_width, n_ssd_heads, q_width, kv_width, kv_width, d, d)
    starts = [0]
    for s in sizes:
        starts.append(starts[-1] + s)
    dt_lo, dt_hi, total = starts[2], starts[3], starts[-1]

    h1, h1b = _ffn_ln(x, ffn1_w_gate.astype(BF16), ffn1_w_up.astype(BF16), ffn1_w_down.astype(BF16),
                      vec(ln1_g), vec(ln1_b), tm=tm_ffn, th=512, emit_bf16=True, name="ffn1_ln")

    w_in_t = w_in.T
    proj_a = _in_proj(h1b, w_in_t, row0=0, n=dt_lo, tm=tm_mm, tn=1024, name="in_proj_a")
    proj_b = _in_proj(h1b, w_in_t, row0=dt_hi, n=total - dt_hi, tm=tm_mm, tn=512, name="in_proj_b")
    dt_t = _dt_proj(h1b, w_in_t, row0=dt_lo, n_heads=n_ssd_heads, tm=tm_mm)

    y_ssd = _ssd(proj_a, dt_t, conv_w, conv_b, dt_bias, a_log, d_skip, ssd_norm_g,
                 d_inner=d_inner, z_col=0, xbc_col=d_inner, rows=min(4 * SSD_CHUNK, t))
    y_attn = _swa(proj_b, positions, attn_sinks, q_col=0, q_width=q_width,
                  k_col=q_width, v_col=q_width + kv_width, kv_width=kv_width)

    merged = _merge(y_ssd, y_attn, w_ssd_o, w_attn_o, proj_b,
                    gs_col=q_width + 2 * kv_width, ga_col=q_width + 2 * kv_width + d, tm=tm_ffn, tn=512)
    h2 = _proj_ln(merged, w_out, h1, vec(ln2_g), vec(ln2_b), tm=tm_ffn)

    (out,) = _ffn_ln(h2, ffn2_w_gate.astype(BF16), ffn2_w_up.astype(BF16), ffn2_w_down.astype(BF16),
                     vec(ln3_g), vec(ln3_b), tm=tm_ffn, th=512, emit_bf16=False, name="ffn2_ln")
    return out


def kernel(x, positions, ffn1_w_gate, ffn1_w_up, ffn1_w_down, ln1_g, ln1_b, w_in, conv_w, conv_b, dt_bias, a_log, d_skip, ssd_norm_g, w_ssd_o, attn_sinks, w_attn_o, w_out, ln2_g, ln2_b, ffn2_w_gate, ffn2_w_up, ffn2_w_down, ln3_g, ln3_b):
    batch, depth = x.shape[0], ffn1_w_gate.shape[0]
    assert depth == DEPTH
    outs = []
    for bi in range(batch):
        h = x[bi]
        for l in range(depth):
            h = _layer(h, positions[bi], ffn1_w_gate[l], ffn1_w_up[l], ffn1_w_down[l], ln1_g[l], ln1_b[l],
                       w_in[l], conv_w[l], conv_b[l], dt_bias[l], a_log[l], d_skip[l], ssd_norm_g[l], w_ssd_o[l],
                       attn_sinks[l], w_attn_o[l], w_out[l], ln2_g[l], ln2_b[l],
                       ffn2_w_gate[l], ffn2_w_up[l], ffn2_w_down[l], ln3_g[l], ln3_b[l])
        outs.append(h)
    return jnp.stack(outs, axis=0)
```

```python
import functools
import math

import jax
import jax.numpy as jnp
from jax import lax
from jax.experimental import pallas as pl
from jax.experimental.pallas import tpu as pltpu

F32 = jnp.float32
BF16 = jnp.bfloat16

V7X_LANES = 128
V7X_SUBLANES = 8
V7X_VMEM_BYTES = 64 * 1024 * 1024

SSD_HEAD_DIM = 64
SSD_N_GROUPS = 8
SSD_D_STATE = 128
SSD_CONV_WIDTH = 4
SSD_CHUNK = 128
ATTN_HEAD_DIM = 64
ATTN_GROUP = 8
ATTN_WINDOW = 128
ROPE_THETA = 10000.0
DEPTH = 1
DEEPNORM_ALPHA = (2 * DEPTH) ** 0.25
LN_EPS = 1e-5
RMS_EPS = 1e-5
LOG2_E = math.log2(math.e)

CAST_ROWS = 256


def _vmem_limit(block_bytes):
    return int(min(block_bytes * 5 // 4 + (8 << 20), V7X_VMEM_BYTES - (6 << 20)))


def _dot(a, b):
    return jnp.dot(a, b, preferred_element_type=F32)


def _dot_nt(a, b):
    return lax.dot_general(a, b, (((1,), (1,)), ((), ())), preferred_element_type=F32)


def _dot_tn(a, b):
    return lax.dot_general(a, b, (((0,), (0,)), ((), ())), preferred_element_type=F32)


def _dot_f32(a, b):
    return jnp.dot(a, b, preferred_element_type=F32, precision=lax.Precision.HIGHEST)


def _layer_norm(y, g, b):
    mu = jnp.mean(y, axis=-1, keepdims=True)
    yc = y - mu
    var = jnp.mean(yc * yc, axis=-1, keepdims=True)
    return yc * lax.rsqrt(var + LN_EPS) * g + b


def _silu(x):
    return x * jax.nn.sigmoid(x)


def _cast_weight(dst_ref, w_ref):
    k = dst_ref.shape[0]
    for r0 in range(0, k, CAST_ROWS):
        rows = slice(r0, min(r0 + CAST_ROWS, k))
        dst_ref[rows, :] = w_ref[rows, :].astype(BF16)


def _ffn_ln_kernel(x_ref, wg_ref, wu_ref, wd_ref, g_ref, b_ref, *refs, n_hidden_tiles, emit_bf16):
    if emit_bf16:
        o_ref, ob_ref, xb_ref, acc_ref = refs
    else:
        o_ref, xb_ref, acc_ref = refs
    j = pl.program_id(1)

    @pl.when(j == 0)
    def _():
        xb_ref[...] = x_ref[...].astype(BF16)
        acc_ref[...] = jnp.zeros_like(acc_ref)

    xb = xb_ref[...]
    gate = _dot(xb, wg_ref[...])
    up = _dot(xb, wu_ref[...])
    act = (_silu(gate) * up).astype(BF16)
    acc_ref[...] += _dot(act, wd_ref[...])

    @pl.when(j == n_hidden_tiles - 1)
    def _():
        y = DEEPNORM_ALPHA * x_ref[...] + 0.5 * acc_ref[...]
        out = _layer_norm(y, g_ref[...], b_ref[...])
        o_ref[...] = out
        if emit_bf16:
            ob_ref[...] = out.astype(BF16)


def _ffn_ln(x, wg, wu, wd, g, b, *, tm, th, emit_bf16, name):
    t, d = x.shape
    hidden = wg.shape[1]
    assert t % tm == 0 and hidden % th == 0
    row = pl.BlockSpec((tm, d), lambda i, j: (i, 0))
    vec = pl.BlockSpec((1, d), lambda i, j: (0, 0))
    out_shape = [jax.ShapeDtypeStruct((t, d), F32)]
    out_specs = [row]
    if emit_bf16:
        out_shape.append(jax.ShapeDtypeStruct((t, d), BF16))
        out_specs.append(row)
    block_bytes = (2 * tm * d * 4 + 2 * 3 * d * th * 2 + 2 * tm * d * 4 + (2 * tm * d * 2 if emit_bf16 else 0)
                   + tm * d * 2 + tm * d * 4)
    return pl.pallas_call(
        functools.partial(_ffn_ln_kernel, n_hidden_tiles=hidden // th, emit_bf16=emit_bf16),
        grid=(t // tm, hidden // th),
        in_specs=[row,
                  pl.BlockSpec((d, th), lambda i, j: (0, j)),
                  pl.BlockSpec((d, th), lambda i, j: (0, j)),
                  pl.BlockSpec((th, d), lambda i, j: (j, 0)),
                  vec, vec],
        out_specs=out_specs,
        out_shape=out_shape,
        scratch_shapes=[pltpu.VMEM((tm, d), BF16), pltpu.VMEM((tm, d), F32)],
        compiler_params=pltpu.CompilerParams(
            dimension_semantics=("parallel", "arbitrary"), vmem_limit_bytes=_vmem_limit(block_bytes)),
        name=name,
    )(x, wg, wu, wd, g, b)


def _in_proj_kernel(a_ref, w_ref, *refs, epilogue):
    if epilogue == "conv_silu":
        cw_ref, cb_ref, o_ref, wb_ref, pad_ref = refs
    else:
        o_ref, wb_ref = refs
    tm = a_ref.shape[0]
    tail = V7X_SUBLANES

    @pl.when(pl.program_id(1) == 0)
    def _():
        _cast_weight(wb_ref, w_ref)
        if epilogue == "conv_silu":
            pad_ref[0:tail, :] = jnp.zeros((tail, pad_ref.shape[1]), F32)

    raw = _dot_nt(a_ref[...], wb_ref[...])
    if epilogue is None:
        o_ref[...] = raw
    elif epilogue == "silu":
        o_ref[...] = _silu(raw)
    else:
        pad_ref[tail:tail + tm, :] = raw
        acc = cb_ref[...]
        for k in range(SSD_CONV_WIDTH):
            start = tail - (SSD_CONV_WIDTH - 1) + k
            acc = acc + cw_ref[k:k + 1, :] * pad_ref[start:start + tm, :]
        pad_ref[0:tail, :] = pad_ref[tm:tm + tail, :]
        o_ref[...] = _silu(acc)


def _in_proj(a, w_t, *, row0, n, tm, tn, weight_buffers, name, epilogue=None, conv_w=None, conv_b=None):
    m, k = a.shape
    assert m % tm == 0 and n % tn == 0 and row0 % V7X_SUBLANES == 0
    w_spec = pl.BlockSpec((pl.Element(tn), pl.Element(k)),
                          lambda j, i: (pl.multiple_of(row0 + j * tn, V7X_SUBLANES), 0),
                          pipeline_mode=pl.Buffered(weight_buffers))
    in_specs = [pl.BlockSpec((tm, k), lambda j, i: (i, 0)), w_spec]
    operands = [a, w_t]
    scratch = [pltpu.VMEM((tn, k), BF16)]
    block_bytes = 2 * (tm * k * 2 + tm * tn * 4) + weight_buffers * tn * k * 4 + tn * k * 2
    if epilogue == "conv_silu":
        in_specs += [pl.BlockSpec((SSD_CONV_WIDTH, tn), lambda j, i: (0, j)),
                     pl.BlockSpec((1, tn), lambda j, i: (0, j))]
        operands += [conv_w, conv_b]
        scratch.append(pltpu.VMEM((tm + V7X_SUBLANES, tn), F32))
        block_bytes += (tm + V7X_SUBLANES) * tn * 4 + tm * tn * 4
    return pl.pallas_call(
        functools.partial(_in_proj_kernel, epilogue=epilogue),
        grid=(n // tn, m // tm),
        in_specs=in_specs,
        out_specs=pl.BlockSpec((tm, tn), lambda j, i: (i, j)),
        out_shape=jax.ShapeDtypeStruct((m, n), F32),
        scratch_shapes=scratch,
        compiler_params=pltpu.CompilerParams(
            dimension_semantics=("parallel", "arbitrary"), vmem_limit_bytes=_vmem_limit(block_bytes)),
        name=name,
    )(*operands)


def _dt_proj_kernel(a_ref, w_ref, o_ref):
    o_ref[...] = _dot_nt(w_ref[...].astype(BF16), a_ref[...])


def _dt_proj(a, w_t, *, row0, n_heads, tm):
    m, k = a.shape
    assert row0 % n_heads == 0 and n_heads % V7X_SUBLANES == 0 and m % tm == 0
    block_bytes = 2 * (tm * k * 2 + n_heads * k * 4 + n_heads * tm * 4)
    return pl.pallas_call(
        _dt_proj_kernel,
        grid=(m // tm,),
        in_specs=[pl.BlockSpec((tm, k), lambda i: (i, 0)),
                  pl.BlockSpec((n_heads, k), lambda i: (row0 // n_heads, 0))],
        out_specs=pl.BlockSpec((n_heads, tm), lambda i: (0, i)),
        out_shape=jax.ShapeDtypeStruct((n_heads, m), F32),
        compiler_params=pltpu.CompilerParams(
            dimension_semantics=("parallel",), vmem_limit_bytes=_vmem_limit(block_bytes)),
        name="dt_proj",
    )(a, w_t)


def _ssd_kernel(xs_ref, b_ref, c_ref, z_ref, dt_ref, bias_ref, alog_ref, dskip_ref, ng_ref,
                y_ref, state_ref):
    q = SSD_CHUNK
    rows = xs_ref.shape[0]
    gw = xs_ref.shape[1]
    heads = gw // SSD_HEAD_DIM
    pairs = heads // 2
    pw = 2 * SSD_HEAD_DIM

    @pl.when(pl.program_id(1) == 0)
    def _():
        state_ref[...] = jnp.zeros_like(state_ref)

    dt_all = jax.nn.softplus(dt_ref[...] + bias_ref[...])
    adt_all = dt_all * (-jnp.exp(alog_ref[...]))

    ri = lax.broadcasted_iota(jnp.int32, (q, q), 0)
    ci = lax.broadcasted_iota(jnp.int32, (q, q), 1)
    causal = ci <= ri
    upper = (ri <= ci).astype(F32)
    lo = lax.broadcasted_iota(jnp.int32, (q, pw), 1) < SSD_HEAD_DIM
    lo_row = lax.broadcasted_iota(jnp.int32, (1, pw), 1) < SSD_HEAD_DIM

    staged = []
    for ck in range(rows // q):
        r = slice(ck * q, (ck + 1) * q)
        dt_r = dt_all[:, r]
        cs_r = _dot_f32(adt_all[:, r], upper)

        def over_lanes(row0, row1):
            stacked = jnp.concatenate([jnp.broadcast_to(row0, (SSD_HEAD_DIM, q)),
                                       jnp.broadcast_to(row1, (SSD_HEAD_DIM, q))], axis=0)
            return stacked.T

        cs_l = [over_lanes(cs_r[h:h + 1, :], cs_r[h:h + 1, :]) for h in range(heads)]

        bm_b = b_ref[r, :].astype(BF16)
        cm_b = c_ref[r, :].astype(BF16)
        cb = _dot_nt(cm_b, bm_b)

        y_parts, din_parts, xdd_parts, cd_parts = [], [], [], []
        for p in range(pairs):
            h0, h1 = 2 * p, 2 * p + 1
            sl = slice(p * pw, (p + 1) * pw)
            xp = xs_ref[r, sl]
            xd = xp * over_lanes(dt_r[h0:h0 + 1, :], dt_r[h1:h1 + 1, :])
            xd_b = xd.astype(BF16)
            y_diag = []
            for h in (h0, h1):
                seg = jnp.where(causal, jnp.exp(cs_l[h] - cs_r[h:h + 1, :]), 0.0)
                y_diag.append(_dot((cb * seg).astype(BF16), xd_b))
            cs_p = jnp.where(lo, cs_l[h0], cs_l[h1])
            cs_end = jnp.where(lo_row, cs_l[h0][q - 1:q, :], cs_l[h1][q - 1:q, :])
            y_parts.append(jnp.where(lo, y_diag[0], y_diag[1]) + dskip_ref[:, sl] * xp)
            din_parts.append(jnp.exp(cs_p))
            xdd_parts.append((xd * jnp.exp(cs_end - cs_p)).astype(BF16))
            cd_parts.append(jnp.exp(cs_end))
        staged.append((cm_b, bm_b, jnp.concatenate(y_parts, axis=1), jnp.concatenate(din_parts, axis=1),
                       jnp.concatenate(xdd_parts, axis=1), jnp.concatenate(cd_parts, axis=1)))

    for ck, (cm_b, bm_b, y_local, decay_in, xdd, chunk_decay) in enumerate(staged):
        r = slice(ck * q, (ck + 1) * q)
        state = state_ref[...]
        y = y_local + _dot(cm_b, state.astype(BF16)) * decay_in
        state_ref[...] = state * chunk_decay + _dot_tn(bm_b, xdd)
        y = y * z_ref[r, :]
        y = y * lax.rsqrt(jnp.mean(y * y, axis=-1, keepdims=True) + RMS_EPS) * ng_ref[...]
        y_ref[r, :] = y.astype(y_ref.dtype)


def _ssd(zs, xbc, dt_t, dt_bias, a_log, d_skip, norm_g, *, d_inner, rows):
    t = zs.shape[0]
    g = SSD_N_GROUPS
    gw = d_inner // g
    hg = gw // SSD_HEAD_DIM
    n = SSD_D_STATE
    assert t % rows == 0 and rows % SSD_CHUNK == 0 and d_inner % n == 0
    assert 2 * hg <= SSD_CHUNK and hg % V7X_SUBLANES == 0
    per_head = lambda v: v.reshape(g * hg, 1)
    dskip_l = jnp.repeat(d_skip, SSD_HEAD_DIM).reshape(1, d_inner)
    ng = norm_g.reshape(1, d_inner)

    head_col = pl.BlockSpec((hg, 1), lambda gi, ci: (gi, 0))
    in_specs = [
        pl.BlockSpec((rows, gw), lambda gi, ci: (ci, gi)),
        pl.BlockSpec((rows, n), lambda gi, ci: (ci, d_inner // n + gi)),
        pl.BlockSpec((rows, n), lambda gi, ci: (ci, d_inner // n + g + gi)),
        pl.BlockSpec((rows, gw), lambda gi, ci: (ci, gi)),
        pl.BlockSpec((hg, rows), lambda gi, ci: (gi, ci)),
        head_col, head_col,
        pl.BlockSpec((1, gw), lambda gi, ci: (0, gi)),
        pl.BlockSpec((1, gw), lambda gi, ci: (0, gi)),
    ]
    block_bytes = 2 * (2 * rows * gw * 4 + 2 * rows * n * 4 + rows * gw * 2) + n * gw * 4
    return pl.pallas_call(
        _ssd_kernel,
        grid=(g, t // rows),
        in_specs=in_specs,
        out_specs=pl.BlockSpec((rows, gw), lambda gi, ci: (ci, gi)),
        out_shape=jax.ShapeDtypeStruct((t, d_inner), BF16),
        scratch_shapes=[pltpu.VMEM((n, gw), F32)],
        compiler_params=pltpu.CompilerParams(
            dimension_semantics=("parallel", "arbitrary"), vmem_limit_bytes=_vmem_limit(block_bytes)),
        name="ssd",
    )(xbc, xbc, xbc, zs, dt_t, per_head(dt_bias), per_head(a_log), dskip_l, ng)


def _swa_kernel(q_ref, k_ref, v_ref, pos_ref, invf_ref, sink_ref, o_ref, kprev_ref, vprev_ref):
    w = ATTN_WINDOW
    dh = ATTN_HEAD_DIM
    half = dh // 2
    lanes = 2 * dh
    n_kv = k_ref.shape[1] // dh
    tiles_per_kv = ATTN_GROUP // 2
    nb = pl.program_id(0)

    @pl.when(nb == 0)
    def _():
        kprev_ref[...] = jnp.zeros_like(kprev_ref)
        vprev_ref[...] = jnp.zeros_like(vprev_ref)

    lane = lax.broadcasted_iota(jnp.int32, (w, lanes), 1)
    first_half = (lane % dh) < half
    lo2 = lax.broadcasted_iota(jnp.int32, (2 * w, lanes), 1) < dh

    ang = pos_ref[...].astype(F32) * invf_ref[...]
    cos = jnp.cos(ang)
    sin = jnp.sin(ang)
    sin = jnp.where(first_half, -sin, sin)
    scale = dh ** -0.5

    def rope(u, c, s):
        partner = jnp.where(first_half, pltpu.roll(u, lanes - half, 1), pltpu.roll(u, half, 1))
        return u * c + partner * s

    kj = lax.broadcasted_iota(jnp.int32, (2 * w, w), 0)
    qi = lax.broadcasted_iota(jnp.int32, (2 * w, w), 1)
    valid = (kj > qi) & (kj <= qi + w) & ((kj >= w) | (nb > 0))
    bias = jnp.where(valid, 0.0, -jnp.inf)

    cos_q, sin_q = cos * (scale * LOG2_E), sin * (scale * LOG2_E)
    q_tiles = [rope(q_ref[:, t * lanes:(t + 1) * lanes], cos_q, sin_q).astype(BF16)
               for t in range(q_ref.shape[1] // lanes)]

    for pt in range(n_kv // 2):
        sl = slice(pt * lanes, (pt + 1) * lanes)
        k_cur = rope(k_ref[:, sl], cos, sin)
        v_cur = v_ref[:, sl]
        k_pair = jnp.concatenate([kprev_ref[:, sl], k_cur], axis=0)
        v_pair = jnp.concatenate([vprev_ref[:, sl], v_cur], axis=0)
        kprev_ref[:, sl] = k_cur
        vprev_ref[:, sl] = v_cur
        k_swap = pltpu.roll(k_pair, dh, 1)
        vt_pair = v_pair.T
        for side in range(2):
            hk = 2 * pt + side
            k_lo = jnp.where(lo2, k_pair if side == 0 else k_swap, 0.0).astype(BF16)
            k_hi = jnp.where(lo2, 0.0, k_swap if side == 0 else k_pair).astype(BF16)
            vt = vt_pair[side * dh:(side + 1) * dh, :].astype(BF16)
            qs = jnp.concatenate(q_tiles[hk * tiles_per_kv:(hk + 1) * tiles_per_kv], axis=0)
            o_parity = []
            for parity, k_sel in enumerate((k_lo, k_hi)):
                st = _dot_nt(k_sel, qs)
                p_parts, den_parts = [], []
                for j in range(tiles_per_kv):
                    head = hk * ATTN_GROUP + 2 * j + parity
                    s = st[:, j * w:(j + 1) * w] + bias
                    sink = sink_ref[0:1, head:head + 1] * LOG2_E
                    m = jnp.maximum(jnp.max(s, axis=0, keepdims=True), sink)
                    p = jnp.exp2(s - m)
                    den_parts.append(jnp.sum(p, axis=0, keepdims=True) + jnp.exp2(sink - m))
                    p_parts.append(p.astype(BF16))
                ot = _dot(vt, jnp.concatenate(p_parts, axis=1))
                o_parity.append(ot / jnp.concatenate(den_parts, axis=1))
            for j in range(tiles_per_kv):
                qt = hk * tiles_per_kv + j
                tile_t = jnp.concatenate([o_parity[0][:, j * w:(j + 1) * w],
                                          o_parity[1][:, j * w:(j + 1) * w]], axis=0)
                o_ref[:, qt * lanes:(qt + 1) * lanes] = tile_t.T.astype(o_ref.dtype)


def _swa(proj, positions, sinks, *, q_col, q_width, k_col, v_col, kv_width):
    t = proj.shape[0]
    w = ATTN_WINDOW
    assert t % w == 0 and q_col % q_width == 0 and k_col % kv_width == 0 and v_col % kv_width == 0
    half = ATTN_HEAD_DIM // 2
    inv_freq = ROPE_THETA ** (-jnp.arange(half, dtype=F32) * 2.0 / ATTN_HEAD_DIM)
    inv_freq = jnp.tile(inv_freq, 2 * V7X_LANES // ATTN_HEAD_DIM).reshape(1, V7X_LANES)
    block_bytes = 2 * (w * q_width * 4 + 2 * w * kv_width * 4 + w * q_width * 2) + 2 * w * kv_width * 4
    return pl.pallas_call(
        _swa_kernel,
        grid=(t // w,),
        in_specs=[
            pl.BlockSpec((w, q_width), lambda i: (i, q_col // q_width)),
            pl.BlockSpec((w, kv_width), lambda i: (i, k_col // kv_width)),
            pl.BlockSpec((w, kv_width), lambda i: (i, v_col // kv_width)),
            pl.BlockSpec((w, 1), lambda i: (i, 0)),
            pl.BlockSpec((1, V7X_LANES), lambda i: (0, 0)),
            pl.BlockSpec((1, sinks.shape[0]), lambda i: (0, 0)),
        ],
        out_specs=pl.BlockSpec((w, q_width), lambda i: (i, 0)),
        out_shape=jax.ShapeDtypeStruct((t, q_width), BF16),
        scratch_shapes=[pltpu.VMEM((w, kv_width), F32), pltpu.VMEM((w, kv_width), F32)],
        compiler_params=pltpu.CompilerParams(
            dimension_semantics=("arbitrary",), vmem_limit_bytes=_vmem_limit(block_bytes)),
        name="swa",
    )(proj, proj, proj, positions.reshape(t, 1), inv_freq, sinks.reshape(1, -1))


def _merge_kernel(ys_ref, ya_ref, ws_ref, wa_ref, gs_ref, ga_ref, o_ref, wsb_ref, wab_ref):
    @pl.when(pl.program_id(1) == 0)
    def _():
        _cast_weight(wsb_ref, ws_ref)
        _cast_weight(wab_ref, wa_ref)

    y_s = _dot(ys_ref[...], wsb_ref[...])
    y_a = _dot(ya_ref[...], wab_ref[...])
    merged = jax.nn.sigmoid(gs_ref[...]) * y_s + jax.nn.sigmoid(ga_ref[...]) * y_a
    o_ref[...] = merged.astype(o_ref.dtype)


def _merge(y_ssd, y_attn, w_ssd_o, w_attn_o, proj, *, gs_col, ga_col, tm, tn):
    t, ks = y_ssd.shape
    ka = y_attn.shape[1]
    d = w_ssd_o.shape[1]
    assert t % tm == 0 and d % tn == 0 and gs_col % tn == 0 and ga_col % tn == 0
    block_bytes = (2 * (tm * ks * 2 + tm * ka * 2 + ks * tn * 4 + ka * tn * 4 + 2 * tm * tn * 4 + tm * tn * 2)
                   + (ks + ka) * tn * 2)
    return pl.pallas_call(
        _merge_kernel,
        grid=(d // tn, t // tm),
        in_specs=[pl.BlockSpec((tm, ks), lambda j, i: (i, 0)),
                  pl.BlockSpec((tm, ka), lambda j, i: (i, 0)),
                  pl.BlockSpec((ks, tn), lambda j, i: (0, j)),
                  pl.BlockSpec((ka, tn), lambda j, i: (0, j)),
                  pl.BlockSpec((tm, tn), lambda j, i: (i, gs_col // tn + j)),
                  pl.BlockSpec((tm, tn), lambda j, i: (i, ga_col // tn + j))],
        out_specs=pl.BlockSpec((tm, tn), lambda j, i: (i, j)),
        out_shape=jax.ShapeDtypeStruct((t, d), BF16),
        scratch_shapes=[pltpu.VMEM((ks, tn), BF16), pltpu.VMEM((ka, tn), BF16)],
        compiler_params=pltpu.CompilerParams(
            dimension_semantics=("parallel", "arbitrary"), vmem_limit_bytes=_vmem_limit(block_bytes)),
        name="merge",
    )(y_ssd, y_attn, w_ssd_o, w_attn_o, proj, proj)


def _proj_ln_kernel(m_ref, w_ref, h_ref, g_ref, b_ref, o_ref, wb_ref):
    @pl.when(pl.program_id(0) == 0)
    def _():
        _cast_weight(wb_ref, w_ref)

    y = DEEPNORM_ALPHA * h_ref[...] + _dot(m_ref[...], wb_ref[...])
    o_ref[...] = _layer_norm(y, g_ref[...], b_ref[...])


def _proj_ln(merged, w_out, h, g, b, *, tm):
    t, d = h.shape
    k = merged.shape[1]
    assert t % tm == 0
    vec = pl.BlockSpec((1, d), lambda i: (0, 0))
    block_bytes = 2 * (tm * k * 2 + 2 * tm * d * 4) + k * d * 4 + k * d * 2
    return pl.pallas_call(
        _proj_ln_kernel,
        grid=(t // tm,),
        in_specs=[pl.BlockSpec((tm, k), lambda i: (i, 0)),
                  pl.BlockSpec((k, d), lambda i: (0, 0), pipeline_mode=pl.Buffered(1)),
                  pl.BlockSpec((tm, d), lambda i: (i, 0)),
                  vec, vec],
        out_specs=pl.BlockSpec((tm, d), lambda i: (i, 0)),
        out_shape=jax.ShapeDtypeStruct((t, d), F32),
        scratch_shapes=[pltpu.VMEM((k, d), BF16)],
        compiler_params=pltpu.CompilerParams(
            dimension_semantics=("arbitrary",), vmem_limit_bytes=_vmem_limit(block_bytes)),
        name="proj_ln",
    )(merged, w_out, h, g, b)


def _layer(x, positions, ffn1_w_gate, ffn1_w_up, ffn1_w_down, ln1_g, ln1_b,
           w_in, conv_w, conv_b, dt_bias, a_log, d_skip, ssd_norm_g, w_ssd_o,
           attn_sinks, w_attn_o, w_out, ln2_g, ln2_b,
           ffn2_w_gate, ffn2_w_up, ffn2_w_down, ln3_g, ln3_b):
    t, d = x.shape
    d_inner = w_ssd_o.shape[0]
    n_ssd_heads = dt_bias.shape[0]
    q_width = w_attn_o.shape[0]
    kv_width = q_width // ATTN_GROUP
    xbc_width = d_inner + 2 * SSD_N_GROUPS * SSD_D_STATE
    vec = lambda v: v.reshape(1, -1)
    tm_ffn = min(512, t)
    tm_mm = min(1024, t)

    sizes = (d_inner, xbc_width, n_ssd_heads, q_width, kv_width, kv_width, d, d)
    starts = [0]
    for s in sizes:
        starts.append(starts[-1] + s)
    dt_lo, dt_hi, total = starts[2], starts[3], starts[-1]

    h1, h1b = _ffn_ln(x, ffn1_w_gate.astype(BF16), ffn1_w_up.astype(BF16), ffn1_w_down.astype(BF16),
                      vec(ln1_g), vec(ln1_b), tm=tm_ffn, th=512, emit_bf16=True, name="ffn1_ln")

    w_in_t = w_in.T
    n_b = total - dt_hi
    zs = _in_proj(h1b, w_in_t, row0=0, n=d_inner, tm=tm_mm, tn=1024, weight_buffers=2, name="in_proj_z",
                  epilogue="silu")
    xbc = _in_proj(h1b, w_in_t, row0=d_inner, n=xbc_width, tm=tm_mm, tn=1024, weight_buffers=2,
                   name="in_proj_xbc", epilogue="conv_silu", conv_w=conv_w, conv_b=conv_b.reshape(1, -1))
    proj_b = _in_proj(h1b, w_in_t, row0=dt_hi, n=n_b, tm=tm_mm, tn=n_b // 4, weight_buffers=1, name="in_proj_b")
    dt_t = _dt_proj(h1b, w_in_t, row0=dt_lo, n_heads=n_ssd_heads, tm=tm_mm)

    y_ssd = _ssd(zs, xbc, dt_t, dt_bias, a_log, d_skip, ssd_norm_g, d_inner=d_inner, rows=min(4 * SSD_CHUNK, t))
    y_attn = _swa(proj_b, positions, attn_sinks, q_col=0, q_width=q_width,
                  k_col=q_width, v_col=q_width + kv_width, kv_width=kv_width)

    merged = _merge(y_ssd, y_attn, w_ssd_o, w_attn_o, proj_b,
                    gs_col=q_width + 2 * kv_width, ga_col=q_width + 2 * kv_width + d, tm=tm_ffn, tn=512)
    h2 = _proj_ln(merged, w_out, h1, vec(ln2_g), vec(ln2_b), tm=tm_ffn)

    (out,) = _ffn_ln(h2, ffn2_w_gate.astype(BF16), ffn2_w_up.astype(BF16), ffn2_w_down.astype(BF16),
                     vec(ln3_g), vec(ln3_b), tm=tm_ffn, th=512, emit_bf16=False, name="ffn2_ln")
    return out


def kernel(x, positions, ffn1_w_gate, ffn1_w_up, ffn1_w_down, ln1_g, ln1_b, w_in, conv_w, conv_b, dt_bias, a_log, d_skip, ssd_norm_g, w_ssd_o, attn_sinks, w_attn_o, w_out, ln2_g, ln2_b, ffn2_w_gate, ffn2_w_up, ffn2_w_down, ln3_g, ln3_b):
    batch, depth = x.shape[0], ffn1_w_gate.shape[0]
    assert depth == DEPTH
    outs = []
    for bi in range(batch):
        h = x[bi]
        for l in range(depth):
            h = _layer(h, positions[bi], ffn1_w_gate[l], ffn1_w_up[l], ffn1_w_down[l], ln1_g[l], ln1_b[l],
                       w_in[l], conv_w[l], conv_b[l], dt_bias[l], a_log[l], d_skip[l], ssd_norm_g[l], w_ssd_o[l],
                       attn_sinks[l], w_attn_o[l], w_out[l], ln2_g[l], ln2_b[l],
                       ffn2_w_gate[l], ffn2_w_up[l], ffn2_w_down[l], ln3_g[l], ln3_b[l])
        outs.append(h)
    return jnp.stack(outs, axis=0)
```

```python
import functools
import math

import jax
import jax.numpy as jnp
from jax import lax
from jax.experimental import pallas as pl
from jax.experimental.pallas import tpu as pltpu

F32 = jnp.float32
BF16 = jnp.bfloat16

V7X_LANES = 128
V7X_SUBLANES = 8
V7X_VMEM_BYTES = 64 * 1024 * 1024

SSD_HEAD_DIM = 64
SSD_N_GROUPS = 8
SSD_D_STATE = 128
SSD_CONV_WIDTH = 4
SSD_CHUNK = 128
ATTN_HEAD_DIM = 64
ATTN_GROUP = 8
ATTN_WINDOW = 128
ROPE_THETA = 10000.0
DEPTH = 1
DEEPNORM_ALPHA = (2 * DEPTH) ** 0.25
LN_EPS = 1e-5
RMS_EPS = 1e-5
LOG2_E = math.log2(math.e)

CAST_ROWS = 256
SSD_ROWS_PER_STEP = 8 * SSD_CHUNK


def _vmem_limit(block_bytes):
    return int(min(block_bytes * 5 // 4 + (8 << 20), V7X_VMEM_BYTES - (6 << 20)))


def _dot(a, b):
    return jnp.dot(a, b, preferred_element_type=F32)


def _dot_nt(a, b):
    return lax.dot_general(a, b, (((1,), (1,)), ((), ())), preferred_element_type=F32)


def _dot_tn(a, b):
    return lax.dot_general(a, b, (((0,), (0,)), ((), ())), preferred_element_type=F32)


def _dot_f32(a, b):
    return jnp.dot(a, b, preferred_element_type=F32, precision=lax.Precision.HIGHEST)


def _layer_norm(y, g, b):
    mu = jnp.mean(y, axis=-1, keepdims=True)
    yc = y - mu
    var = jnp.mean(yc * yc, axis=-1, keepdims=True)
    return yc * lax.rsqrt(var + LN_EPS) * g + b


def _silu(x):
    return x * jax.nn.sigmoid(x)


def _cast_weight(dst_ref, w_ref):
    k = dst_ref.shape[0]
    for r0 in range(0, k, CAST_ROWS):
        rows = slice(r0, min(r0 + CAST_ROWS, k))
        dst_ref[rows, :] = w_ref[rows, :].astype(BF16)


def _ffn_ln_kernel(x_ref, wg_ref, wu_ref, wd_ref, g_ref, b_ref, *refs, n_hidden_tiles, emit_bf16, n_side):
    side_in, refs = refs[:n_side], refs[n_side:]
    if emit_bf16:
        o_ref, ob_ref = refs[:2]
        refs = refs[2:]
    else:
        o_ref = refs[0]
        refs = refs[1:]
    side_out, (xb_ref, acc_ref) = refs[:n_side], refs[n_side:]
    j = pl.program_id(1)

    for src_ref, dst_ref in zip(side_in, side_out):
        dst_ref[...] = src_ref[...].astype(BF16)

    @pl.when(j == 0)
    def _():
        xb_ref[...] = x_ref[...].astype(BF16)
        acc_ref[...] = jnp.zeros_like(acc_ref)

    xb = xb_ref[...]
    gate = _dot(xb, wg_ref[...])
    up = _dot(xb, wu_ref[...])
    act = (_silu(gate) * up).astype(BF16)
    acc_ref[...] += _dot(act, wd_ref[...])

    @pl.when(j == n_hidden_tiles - 1)
    def _():
        y = DEEPNORM_ALPHA * x_ref[...] + 0.5 * acc_ref[...]
        out = _layer_norm(y, g_ref[...], b_ref[...])
        o_ref[...] = out
        if emit_bf16:
            ob_ref[...] = out.astype(BF16)


def _ffn_ln(x, wg, wu, wd, g, b, *, tm, th, emit_bf16, name, side_cast=()):
    t, d = x.shape
    hidden = wg.shape[1]
    assert t % tm == 0 and hidden % th == 0
    n_i, n_j = t // tm, hidden // th
    row = pl.BlockSpec((tm, d), lambda i, j: (i, 0))
    vec = pl.BlockSpec((1, d), lambda i, j: (0, 0))
    out_shape = [jax.ShapeDtypeStruct((t, d), F32)]
    out_specs = [row]
    if emit_bf16:
        out_shape.append(jax.ShapeDtypeStruct((t, d), BF16))
        out_specs.append(row)
    block_bytes = (2 * tm * d * 4 + 2 * 3 * d * th * 2 + 2 * tm * d * 4 + (2 * tm * d * 2 if emit_bf16 else 0)
                   + tm * d * 2 + tm * d * 4)
    side_specs = []
    for w in side_cast:
        r, c = w.shape
        if r % n_i == 0 and c % n_j == 0 and (c // n_j) % V7X_LANES == 0:
            spec = pl.BlockSpec((r // n_i, c // n_j), lambda i, j: (i, j))
        else:
            assert r % n_j == 0 and c % n_i == 0 and (c // n_i) % V7X_LANES == 0
            spec = pl.BlockSpec((r // n_j, c // n_i), lambda i, j: (j, i))
        side_specs.append(spec)
        out_shape.append(jax.ShapeDtypeStruct((r, c), BF16))
        block_bytes += 2 * (r * c // (n_i * n_j)) * (4 + 2)
    return pl.pallas_call(
        functools.partial(_ffn_ln_kernel, n_hidden_tiles=n_j, emit_bf16=emit_bf16, n_side=len(side_cast)),
        grid=(n_i, n_j),
        in_specs=[row,
                  pl.BlockSpec((d, th), lambda i, j: (0, j)),
                  pl.BlockSpec((d, th), lambda i, j: (0, j)),
                  pl.BlockSpec((th, d), lambda i, j: (j, 0)),
                  vec, vec] + side_specs,
        out_specs=out_specs + side_specs,
        out_shape=out_shape,
        scratch_shapes=[pltpu.VMEM((tm, d), BF16), pltpu.VMEM((tm, d), F32)],
        compiler_params=pltpu.CompilerParams(
            dimension_semantics=("parallel", "arbitrary"), vmem_limit_bytes=_vmem_limit(block_bytes)),
        name=name,
    )(x, wg, wu, wd, g, b, *side_cast)


def _in_proj_kernel(a_ref, w_ref, o_ref, wb_ref, *, apply_silu):
    @pl.when(pl.program_id(1) == 0)
    def _():
        _cast_weight(wb_ref, w_ref)

    raw = _dot_nt(a_ref[...], wb_ref[...])
    o_ref[...] = _silu(raw) if apply_silu else raw


def _in_proj(a, w_t, *, row0, n, tm, tn, weight_buffers, name, apply_silu=False):
    m, k = a.shape
    assert m % tm == 0 and n % tn == 0 and row0 % V7X_SUBLANES == 0
    w_spec = pl.BlockSpec((pl.Element(tn), pl.Element(k)),
                          lambda j, i: (pl.multiple_of(row0 + j * tn, V7X_SUBLANES), 0),
                          pipeline_mode=pl.Buffered(weight_buffers))
    block_bytes = 2 * (tm * k * 2 + tm * tn * 4) + weight_buffers * tn * k * 4 + tn * k * 2
    return pl.pallas_call(
        functools.partial(_in_proj_kernel, apply_silu=apply_silu),
        grid=(n // tn, m // tm),
        in_specs=[pl.BlockSpec((tm, k), lambda j, i: (i, 0)), w_spec],
        out_specs=pl.BlockSpec((tm, tn), lambda j, i: (i, j)),
        out_shape=jax.ShapeDtypeStruct((m, n), F32),
        scratch_shapes=[pltpu.VMEM((tn, k), BF16)],
        compiler_params=pltpu.CompilerParams(
            dimension_semantics=("parallel", "arbitrary"), vmem_limit_bytes=_vmem_limit(block_bytes)),
        name=name,
    )(a, w_t)


def _dt_proj_kernel(a_ref, w_ref, o_ref):
    o_ref[...] = _dot_nt(w_ref[...].astype(BF16), a_ref[...])


def _dt_proj(a, w_t, *, row0, n_heads, tm):
    m, k = a.shape
    assert row0 % n_heads == 0 and n_heads % V7X_SUBLANES == 0 and m % tm == 0
    block_bytes = 2 * (tm * k * 2 + n_heads * k * 4 + n_heads * tm * 4)
    return pl.pallas_call(
        _dt_proj_kernel,
        grid=(m // tm,),
        in_specs=[pl.BlockSpec((tm, k), lambda i: (i, 0)),
                  pl.BlockSpec((n_heads, k), lambda i: (row0 // n_heads, 0))],
        out_specs=pl.BlockSpec((n_heads, tm), lambda i: (0, i)),
        out_shape=jax.ShapeDtypeStruct((n_heads, m), F32),
        compiler_params=pltpu.CompilerParams(
            dimension_semantics=("parallel",), vmem_limit_bytes=_vmem_limit(block_bytes)),
        name="dt_proj",
    )(a, w_t)


def _ssd_kernel(x_ref, b_ref, c_ref, z_ref, dt_ref,
                cwx_ref, cwb_ref, cwc_ref, cbx_ref, cbb_ref, cbc_ref,
                bias_ref, alog_ref, dskip_ref, ng_ref,
                y_ref, xpad_ref, bpad_ref, cpad_ref, xs_ref, bm_ref, cm_ref, state_ref):
    q = SSD_CHUNK
    tail = V7X_SUBLANES
    rows = x_ref.shape[0]
    gw = x_ref.shape[1]
    heads = gw // SSD_HEAD_DIM
    pairs = heads // 2
    pw = 2 * SSD_HEAD_DIM

    @pl.when(pl.program_id(1) == 0)
    def _():
        xpad_ref[0:tail, :] = jnp.zeros((tail, xpad_ref.shape[1]), F32)
        bpad_ref[0:tail, :] = jnp.zeros((tail, bpad_ref.shape[1]), F32)
        cpad_ref[0:tail, :] = jnp.zeros((tail, cpad_ref.shape[1]), F32)
        state_ref[...] = jnp.zeros_like(state_ref)

    def conv_silu(u_ref, pad_ref, w_ref, cbias_ref, dst_ref):
        pad_ref[tail:tail + rows, :] = u_ref[...]
        acc = cbias_ref[...]
        for k in range(SSD_CONV_WIDTH):
            start = tail - (SSD_CONV_WIDTH - 1) + k
            acc = acc + w_ref[k:k + 1, :] * pad_ref[start:start + rows, :]
        pad_ref[0:tail, :] = pad_ref[rows:rows + tail, :]
        dst_ref[...] = _silu(acc).astype(dst_ref.dtype)

    conv_silu(x_ref, xpad_ref, cwx_ref, cbx_ref, xs_ref)
    conv_silu(b_ref, bpad_ref, cwb_ref, cbb_ref, bm_ref)
    conv_silu(c_ref, cpad_ref, cwc_ref, cbc_ref, cm_ref)

    dt_all = jax.nn.softplus(dt_ref[...] + bias_ref[...])
    adt_all = dt_all * (-jnp.exp(alog_ref[...]))

    ri = lax.broadcasted_iota(jnp.int32, (q, q), 0)
    ci = lax.broadcasted_iota(jnp.int32, (q, q), 1)
    causal = ci <= ri
    upper = (ri <= ci).astype(F32)
    lo = lax.broadcasted_iota(jnp.int32, (q, pw), 1) < SSD_HEAD_DIM
    lo_row = lax.broadcasted_iota(jnp.int32, (1, pw), 1) < SSD_HEAD_DIM

    staged = []
    for ck in range(rows // q):
        r = slice(ck * q, (ck + 1) * q)
        dt_r = dt_all[:, r]
        cs_r = _dot_f32(adt_all[:, r], upper)

        def over_lanes(row0, row1):
            stacked = jnp.concatenate([jnp.broadcast_to(row0, (SSD_HEAD_DIM, q)),
                                       jnp.broadcast_to(row1, (SSD_HEAD_DIM, q))], axis=0)
            return stacked.T

        cs_l = [over_lanes(cs_r[h:h + 1, :], cs_r[h:h + 1, :]) for h in range(heads)]

        bm_b = bm_ref[r, :]
        cm_b = cm_ref[r, :]
        cb = _dot_nt(cm_b, bm_b)

        y_parts, din_parts, xdd_parts, cd_parts = [], [], [], []
        for p in range(pairs):
            h0, h1 = 2 * p, 2 * p + 1
            sl = slice(p * pw, (p + 1) * pw)
            xp = xs_ref[r, sl]
            xd = xp * over_lanes(dt_r[h0:h0 + 1, :], dt_r[h1:h1 + 1, :])
            xd_b = xd.astype(BF16)
            y_diag = []
            for h in (h0, h1):
                seg = jnp.where(causal, jnp.exp(cs_l[h] - cs_r[h:h + 1, :]), 0.0)
                y_diag.append(_dot((cb * seg).astype(BF16), xd_b))
            cs_p = jnp.where(lo, cs_l[h0], cs_l[h1])
            cs_end = jnp.where(lo_row, cs_l[h0][q - 1:q, :], cs_l[h1][q - 1:q, :])
            y_parts.append(jnp.where(lo, y_diag[0], y_diag[1]) + dskip_ref[:, sl] * xp)
            din_parts.append(jnp.exp(cs_p))
            xdd_parts.append((xd * jnp.exp(cs_end - cs_p)).astype(BF16))
            cd_parts.append(jnp.exp(cs_end))
        staged.append((cm_b, bm_b, jnp.concatenate(y_parts, axis=1), jnp.concatenate(din_parts, axis=1),
                       jnp.concatenate(xdd_parts, axis=1), jnp.concatenate(cd_parts, axis=1)))

    for ck, (cm_b, bm_b, y_local, decay_in, xdd, chunk_decay) in enumerate(staged):
        r = slice(ck * q, (ck + 1) * q)
        state = state_ref[...]
        y = y_local + _dot(cm_b, state.astype(BF16)) * decay_in
        state_ref[...] = state * chunk_decay + _dot_tn(bm_b, xdd)
        y = y * z_ref[r, :]
        y = y * lax.rsqrt(jnp.mean(y * y, axis=-1, keepdims=True) + RMS_EPS) * ng_ref[...]
        y_ref[r, :] = y.astype(y_ref.dtype)


def _ssd(zs, xbc, dt_t, conv_w, conv_b, dt_bias, a_log, d_skip, norm_g, *, d_inner, rows):
    t = zs.shape[0]
    g = SSD_N_GROUPS
    gw = d_inner // g
    hg = gw // SSD_HEAD_DIM
    n = SSD_D_STATE
    assert t % rows == 0 and rows % SSD_CHUNK == 0 and d_inner % n == 0
    assert 2 * hg <= SSD_CHUNK and hg % V7X_SUBLANES == 0
    conv_b2 = conv_b.reshape(1, -1)
    per_head = lambda v: v.reshape(g * hg, 1)
    dskip_l = jnp.repeat(d_skip, SSD_HEAD_DIM).reshape(1, d_inner)
    ng = norm_g.reshape(1, d_inner)
    b_blk = d_inner // n
    c_blk = b_blk + g

    head_col = pl.BlockSpec((hg, 1), lambda gi, ci: (gi, 0))
    in_specs = [
        pl.BlockSpec((rows, gw), lambda gi, ci: (ci, gi)),
        pl.BlockSpec((rows, n), lambda gi, ci: (ci, b_blk + gi)),
        pl.BlockSpec((rows, n), lambda gi, ci: (ci, c_blk + gi)),
        pl.BlockSpec((rows, gw), lambda gi, ci: (ci, gi)),
        pl.BlockSpec((hg, rows), lambda gi, ci: (gi, ci)),
        pl.BlockSpec((SSD_CONV_WIDTH, gw), lambda gi, ci: (0, gi)),
        pl.BlockSpec((SSD_CONV_WIDTH, n), lambda gi, ci: (0, b_blk + gi)),
        pl.BlockSpec((SSD_CONV_WIDTH, n), lambda gi, ci: (0, c_blk + gi)),
        pl.BlockSpec((1, gw), lambda gi, ci: (0, gi)),
        pl.BlockSpec((1, n), lambda gi, ci: (0, b_blk + gi)),
        pl.BlockSpec((1, n), lambda gi, ci: (0, c_blk + gi)),
        head_col, head_col,
        pl.BlockSpec((1, gw), lambda gi, ci: (0, gi)),
        pl.BlockSpec((1, gw), lambda gi, ci: (0, gi)),
    ]
    pad_rows = rows + V7X_SUBLANES
    block_bytes = (2 * (2 * rows * gw * 4 + 2 * rows * n * 4 + rows * gw * 2) + pad_rows * (gw + 2 * n) * 4
                   + rows * gw * 4 + 2 * rows * n * 2 + n * gw * 4)
    return pl.pallas_call(
        _ssd_kernel,
        grid=(g, t // rows),
        in_specs=in_specs,
        out_specs=pl.BlockSpec((rows, gw), lambda gi, ci: (ci, gi)),
        out_shape=jax.ShapeDtypeStruct((t, d_inner), BF16),
        scratch_shapes=[pltpu.VMEM((pad_rows, gw), F32), pltpu.VMEM((pad_rows, n), F32),
                        pltpu.VMEM((pad_rows, n), F32), pltpu.VMEM((rows, gw), F32),
                        pltpu.VMEM((rows, n), BF16), pltpu.VMEM((rows, n), BF16),
                        pltpu.VMEM((n, gw), F32)],
        compiler_params=pltpu.CompilerParams(
            dimension_semantics=("parallel", "arbitrary"), vmem_limit_bytes=_vmem_limit(block_bytes)),
        name="ssd",
    )(xbc, xbc, xbc, zs, dt_t, conv_w, conv_w, conv_w, conv_b2, conv_b2, conv_b2,
      per_head(dt_bias), per_head(a_log), dskip_l, ng)


def _swa_kernel(q_ref, k_ref, v_ref, pos_ref, invf_ref, sink_ref, o_ref, kprev_ref, vprev_ref):
    w = ATTN_WINDOW
    dh = ATTN_HEAD_DIM
    half = dh // 2
    lanes = 2 * dh
    n_kv = k_ref.shape[1] // dh
    tiles_per_kv = ATTN_GROUP // 2
    nb = pl.program_id(0)

    @pl.when(nb == 0)
    def _():
        kprev_ref[...] = jnp.zeros_like(kprev_ref)
        vprev_ref[...] = jnp.zeros_like(vprev_ref)

    lane = lax.broadcasted_iota(jnp.int32, (w, lanes), 1)
    first_half = (lane % dh) < half
    lo2 = lax.broadcasted_iota(jnp.int32, (2 * w, lanes), 1) < dh

    ang = pos_ref[...].astype(F32) * invf_ref[...]
    cos = jnp.cos(ang)
    sin = jnp.sin(ang)
    sin = jnp.where(first_half, -sin, sin)
    scale = dh ** -0.5

    def rope(u, c, s):
        partner = jnp.where(first_half, pltpu.roll(u, lanes - half, 1), pltpu.roll(u, half, 1))
        return u * c + partner * s

    kj = lax.broadcasted_iota(jnp.int32, (2 * w, w), 0)
    qi = lax.broadcasted_iota(jnp.int32, (2 * w, w), 1)
    valid = (kj > qi) & (kj <= qi + w) & ((kj >= w) | (nb > 0))
    bias = jnp.where(valid, 0.0, -jnp.inf)

    cos_q, sin_q = cos * (scale * LOG2_E), sin * (scale * LOG2_E)
    q_tiles = [rope(q_ref[:, t * lanes:(t + 1) * lanes], cos_q, sin_q).astype(BF16)
               for t in range(q_ref.shape[1] // lanes)]

    for pt in range(n_kv // 2):
        sl = slice(pt * lanes, (pt + 1) * lanes)
        k_cur = rope(k_ref[:, sl], cos, sin)
        v_cur = v_ref[:, sl]
        k_pair = jnp.concatenate([kprev_ref[:, sl], k_cur], axis=0)
        v_pair = jnp.concatenate([vprev_ref[:, sl], v_cur], axis=0)
        kprev_ref[:, sl] = k_cur
        vprev_ref[:, sl] = v_cur
        k_swap = pltpu.roll(k_pair, dh, 1)
        vt_pair = v_pair.T
        for side in range(2):
            hk = 2 * pt + side
            k_lo = jnp.where(lo2, k_pair if side == 0 else k_swap, 0.0).astype(BF16)
            k_hi = jnp.where(lo2, 0.0, k_swap if side == 0 else k_pair).astype(BF16)
            vt = vt_pair[side * dh:(side + 1) * dh, :].astype(BF16)
            qs = jnp.concatenate(q_tiles[hk * tiles_per_kv:(hk + 1) * tiles_per_kv], axis=0)
            o_parity = []
            for parity, k_sel in enumerate((k_lo, k_hi)):
                st = _dot_nt(k_sel, qs)
                p_parts, den_parts = [], []
                for j in range(tiles_per_kv):
                    head = hk * ATTN_GROUP + 2 * j + parity
                    s = st[:, j * w:(j + 1) * w] + bias
                    sink = sink_ref[0:1, head:head + 1] * LOG2_E
                    m = jnp.maximum(jnp.max(s, axis=0, keepdims=True), sink)
                    p = jnp.exp2(s - m)
                    den_parts.append(jnp.sum(p, axis=0, keepdims=True) + jnp.exp2(sink - m))
                    p_parts.append(p.astype(BF16))
                ot = _dot(vt, jnp.concatenate(p_parts, axis=1))
                o_parity.append(ot / jnp.concatenate(den_parts, axis=1))
            for j in range(tiles_per_kv):
                qt = hk * tiles_per_kv + j
                tile_t = jnp.concatenate([o_parity[0][:, j * w:(j + 1) * w],
                                          o_parity[1][:, j * w:(j + 1) * w]], axis=0)
                o_ref[:, qt * lanes:(qt + 1) * lanes] = tile_t.T.astype(o_ref.dtype)


def _swa(proj, positions, sinks, *, q_col, q_width, k_col, v_col, kv_width):
    t = proj.shape[0]
    w = ATTN_WINDOW
    assert t % w == 0 and q_col % q_width == 0 and k_col % kv_width == 0 and v_col % kv_width == 0
    half = ATTN_HEAD_DIM // 2
    inv_freq = ROPE_THETA ** (-jnp.arange(half, dtype=F32) * 2.0 / ATTN_HEAD_DIM)
    inv_freq = jnp.tile(inv_freq, 2 * V7X_LANES // ATTN_HEAD_DIM).reshape(1, V7X_LANES)
    block_bytes = 2 * (w * q_width * 4 + 2 * w * kv_width * 4 + w * q_width * 2) + 2 * w * kv_width * 4
    return pl.pallas_call(
        _swa_kernel,
        grid=(t // w,),
        in_specs=[
            pl.BlockSpec((w, q_width), lambda i: (i, q_col // q_width)),
            pl.BlockSpec((w, kv_width), lambda i: (i, k_col // kv_width)),
            pl.BlockSpec((w, kv_width), lambda i: (i, v_col // kv_width)),
            pl.BlockSpec((w, 1), lambda i: (i, 0)),
            pl.BlockSpec((1, V7X_LANES), lambda i: (0, 0)),
            pl.BlockSpec((1, sinks.shape[0]), lambda i: (0, 0)),
        ],
        out_specs=pl.BlockSpec((w, q_width), lambda i: (i, 0)),
        out_shape=jax.ShapeDtypeStruct((t, q_width), BF16),
        scratch_shapes=[pltpu.VMEM((w, kv_width), F32), pltpu.VMEM((w, kv_width), F32)],
        compiler_params=pltpu.CompilerParams(
            dimension_semantics=("arbitrary",), vmem_limit_bytes=_vmem_limit(block_bytes)),
        name="swa",
    )(proj, proj, proj, positions.reshape(t, 1), inv_freq, sinks.reshape(1, -1))


def _merge_kernel(ys_ref, ya_ref, ws_ref, wa_ref, gs_ref, ga_ref, o_ref, wsb_ref, wab_ref):
    @pl.when(pl.program_id(1) == 0)
    def _():
        _cast_weight(wsb_ref, ws_ref)
        _cast_weight(wab_ref, wa_ref)

    y_s = _dot(ys_ref[...], wsb_ref[...])
    y_a = _dot(ya_ref[...], wab_ref[...])
    merged = jax.nn.sigmoid(gs_ref[...]) * y_s + jax.nn.sigmoid(ga_ref[...]) * y_a
    o_ref[...] = merged.astype(o_ref.dtype)


def _merge(y_ssd, y_attn, w_ssd_o, w_attn_o, proj, *, gs_col, ga_col, tm, tn):
    t, ks = y_ssd.shape
    ka = y_attn.shape[1]
    d = w_ssd_o.shape[1]
    assert t % tm == 0 and d % tn == 0 and gs_col % tn == 0 and ga_col % tn == 0
    block_bytes = (2 * (tm * ks * 2 + tm * ka * 2 + ks * tn * 4 + ka * tn * 4 + 2 * tm * tn * 4 + tm * tn * 2)
                   + (ks + ka) * tn * 2)
    return pl.pallas_call(
        _merge_kernel,
        grid=(d // tn, t // tm),
        in_specs=[pl.BlockSpec((tm, ks), lambda j, i: (i, 0)),
                  pl.BlockSpec((tm, ka), lambda j, i: (i, 0)),
                  pl.BlockSpec((ks, tn), lambda j, i: (0, j)),
                  pl.BlockSpec((ka, tn), lambda j, i: (0, j)),
                  pl.BlockSpec((tm, tn), lambda j, i: (i, gs_col // tn + j)),
                  pl.BlockSpec((tm, tn), lambda j, i: (i, ga_col // tn + j))],
        out_specs=pl.BlockSpec((tm, tn), lambda j, i: (i, j)),
        out_shape=jax.ShapeDtypeStruct((t, d), BF16),
        scratch_shapes=[pltpu.VMEM((ks, tn), BF16), pltpu.VMEM((ka, tn), BF16)],
        compiler_params=pltpu.CompilerParams(
            dimension_semantics=("parallel", "arbitrary"), vmem_limit_bytes=_vmem_limit(block_bytes)),
        name="merge",
    )(y_ssd, y_attn, w_ssd_o, w_attn_o, proj, proj)


def _proj_ln_kernel(m_ref, w_ref, h_ref, g_ref, b_ref, o_ref, wb_ref):
    @pl.when(pl.program_id(0) == 0)
    def _():
        _cast_weight(wb_ref, w_ref)

    y = DEEPNORM_ALPHA * h_ref[...] + _dot(m_ref[...], wb_ref[...])
    o_ref[...] = _layer_norm(y, g_ref[...], b_ref[...])


def _proj_ln(merged, w_out, h, g, b, *, tm):
    t, d = h.shape
    k = merged.shape[1]
    assert t % tm == 0
    vec = pl.BlockSpec((1, d), lambda i: (0, 0))
    block_bytes = 2 * (tm * k * 2 + 2 * tm * d * 4) + k * d * 4 + k * d * 2
    return pl.pallas_call(
        _proj_ln_kernel,
        grid=(t // tm,),
        in_specs=[pl.BlockSpec((tm, k), lambda i: (i, 0)),
                  pl.BlockSpec((k, d), lambda i: (0, 0), pipeline_mode=pl.Buffered(1)),
                  pl.BlockSpec((tm, d), lambda i: (i, 0)),
                  vec, vec],
        out_specs=pl.BlockSpec((tm, d), lambda i: (i, 0)),
        out_shape=jax.ShapeDtypeStruct((t, d), F32),
        scratch_shapes=[pltpu.VMEM((k, d), BF16)],
        compiler_params=pltpu.CompilerParams(
            dimension_semantics=("arbitrary",), vmem_limit_bytes=_vmem_limit(block_bytes)),
        name="proj_ln",
    )(merged, w_out, h, g, b)


def _layer(x, positions, ffn1_w_gate, ffn1_w_up, ffn1_w_down, ln1_g, ln1_b,
           w_in, conv_w, conv_b, dt_bias, a_log, d_skip, ssd_norm_g, w_ssd_o,
           attn_sinks, w_attn_o, w_out, ln2_g, ln2_b,
           ffn2_w_gate, ffn2_w_up, ffn2_w_down, ln3_g, ln3_b):
    t, d = x.shape
    d_inner = w_ssd_o.shape[0]
    n_ssd_heads = dt_bias.shape[0]
    q_width = w_attn_o.shape[0]
    kv_width = q_width // ATTN_GROUP
    xbc_width = d_inner + 2 * SSD_N_GROUPS * SSD_D_STATE
    vec = lambda v: v.reshape(1, -1)
    tm_ffn = min(512, t)
    tm_mm = min(1024, t)

    sizes = (d_inner, xbc_width, n_ssd_heads, q_width, kv_width, kv_width, d, d)
    starts = [0]
    for s in sizes:
        starts.append(starts[-1] + s)
    dt_lo, dt_hi, total = starts[2], starts[3], starts[-1]

    h1, h1b, wg2, wu2, wd2 = _ffn_ln(
        x, ffn1_w_gate.astype(BF16), ffn1_w_up.astype(BF16), ffn1_w_down.astype(BF16), vec(ln1_g), vec(ln1_b),
        tm=tm_ffn, th=512, emit_bf16=True, name="ffn1_ln", side_cast=(ffn2_w_gate, ffn2_w_up, ffn2_w_down))

    w_in_t = w_in.T
    n_b = total - dt_hi
    zs = _in_proj(h1b, w_in_t, row0=0, n=d_inner, tm=tm_mm, tn=1024, weight_buffers=2, name="in_proj_z",
                  apply_silu=True)
    xbc = _in_proj(h1b, w_in_t, row0=d_inner, n=xbc_width, tm=tm_mm, tn=1024, weight_buffers=2,
                   name="in_proj_xbc")
    proj_b = _in_proj(h1b, w_in_t, row0=dt_hi, n=n_b, tm=tm_mm, tn=n_b // 4, weight_buffers=1, name="in_proj_b")
    dt_t = _dt_proj(h1b, w_in_t, row0=dt_lo, n_heads=n_ssd_heads, tm=tm_mm)

    y_ssd = _ssd(zs, xbc, dt_t, conv_w, conv_b, dt_bias, a_log, d_skip, ssd_norm_g,
                 d_inner=d_inner, rows=min(SSD_ROWS_PER_STEP, t))
    y_attn = _swa(proj_b, positions, attn_sinks, q_col=0, q_width=q_width,
                  k_col=q_width, v_col=q_width + kv_width, kv_width=kv_width)

    merged = _merge(y_ssd, y_attn, w_ssd_o, w_attn_o, proj_b,
                    gs_col=q_width + 2 * kv_width, ga_col=q_width + 2 * kv_width + d, tm=tm_ffn, tn=512)
    h2 = _proj_ln(merged, w_out, h1, vec(ln2_g), vec(ln2_b), tm=tm_ffn)

    (out,) = _ffn_ln(h2, wg2, wu2, wd2, vec(ln3_g), vec(ln3_b), tm=tm_ffn, th=512, emit_bf16=False, name="ffn2_ln")
    return out


def kernel(x, positions, ffn1_w_gate, ffn1_w_up, ffn1_w_down, ln1_g, ln1_b, w_in, conv_w, conv_b, dt_bias, a_log, d_skip, ssd_norm_g, w_ssd_o, attn_sinks, w_attn_o, w_out, ln2_g, ln2_b, ffn2_w_gate, ffn2_w_up, ffn2_w_down, ln3_g, ln3_b):
    batch, depth = x.shape[0], ffn1_w_gate.shape[0]
    assert depth == DEPTH
    outs = []
    for bi in range(batch):
        h = x[bi]
        for l in range(depth):
            h = _layer(h, positions[bi], ffn1_w_gate[l], ffn1_w_up[l], ffn1_w_down[l], ln1_g[l], ln1_b[l],
                       w_in[l], conv_w[l], conv_b[l], dt_bias[l], a_log[l], d_skip[l], ssd_norm_g[l], w_ssd_o[l],
                       attn_sinks[l], w_attn_o[l], w_out[l], ln2_g[l], ln2_b[l],
                       ffn2_w_gate[l], ffn2_w_up[l], ffn2_w_down[l], ln3_g[l], ln3_b[l])
        outs.append(h)
    return jnp.stack(outs, axis=0)
```

```python
import functools
import math

import jax
import jax.numpy as jnp
from jax import lax
from jax.experimental import pallas as pl
from jax.experimental.pallas import tpu as pltpu

F32 = jnp.float32
BF16 = jnp.bfloat16

V7X_LANES = 128
V7X_SUBLANES = 8
V7X_VMEM_BYTES = 64 * 1024 * 1024

SSD_HEAD_DIM = 64
SSD_N_GROUPS = 8
SSD_D_STATE = 128
SSD_CONV_WIDTH = 4
SSD_CHUNK = 128
ATTN_HEAD_DIM = 64
ATTN_GROUP = 8
ATTN_WINDOW = 128
ROPE_THETA = 10000.0
DEPTH = 1
DEEPNORM_ALPHA = (2 * DEPTH) ** 0.25
LN_EPS = 1e-5
RMS_EPS = 1e-5
LOG2_E = math.log2(math.e)

CAST_ROWS = 256
SSD_ROWS_PER_STEP = 8 * SSD_CHUNK


def _vmem_limit(block_bytes):
    return int(min(block_bytes * 5 // 4 + (8 << 20), V7X_VMEM_BYTES - (6 << 20)))


def _dot(a, b):
    return jnp.dot(a, b, preferred_element_type=F32)


def _dot_nt(a, b):
    return lax.dot_general(a, b, (((1,), (1,)), ((), ())), preferred_element_type=F32)


def _dot_tn(a, b):
    return lax.dot_general(a, b, (((0,), (0,)), ((), ())), preferred_element_type=F32)


def _dot_f32(a, b):
    return jnp.dot(a, b, preferred_element_type=F32, precision=lax.Precision.HIGHEST)


def _layer_norm(y, g, b):
    mu = jnp.mean(y, axis=-1, keepdims=True)
    yc = y - mu
    var = jnp.mean(yc * yc, axis=-1, keepdims=True)
    return yc * lax.rsqrt(var + LN_EPS) * g + b


def _silu(x):
    return x * jax.nn.sigmoid(x)


def _cast_weight(dst_ref, w_ref):
    k = dst_ref.shape[0]
    for r0 in range(0, k, CAST_ROWS):
        rows = slice(r0, min(r0 + CAST_ROWS, k))
        dst_ref[rows, :] = w_ref[rows, :].astype(BF16)


def _ffn_ln_kernel(x_ref, wg_ref, wu_ref, wd_ref, g_ref, b_ref, *refs, n_hidden_tiles, emit_bf16, n_side):
    side_in, refs = refs[:n_side], refs[n_side:]
    if emit_bf16:
        o_ref, ob_ref = refs[:2]
        refs = refs[2:]
    else:
        o_ref = refs[0]
        refs = refs[1:]
    side_out, (xb_ref, acc_ref) = refs[:n_side], refs[n_side:]
    j = pl.program_id(1)

    for src_ref, dst_ref in zip(side_in, side_out):
        dst_ref[...] = src_ref[...].astype(BF16)

    @pl.when(j == 0)
    def _():
        xb_ref[...] = x_ref[...].astype(BF16)
        acc_ref[...] = jnp.zeros_like(acc_ref)

    xb = xb_ref[...]
    gate = _dot(xb, wg_ref[...])
    up = _dot(xb, wu_ref[...])
    act = (_silu(gate) * up).astype(BF16)
    acc_ref[...] += _dot(act, wd_ref[...])

    @pl.when(j == n_hidden_tiles - 1)
    def _():
        y = DEEPNORM_ALPHA * x_ref[...] + 0.5 * acc_ref[...]
        out = _layer_norm(y, g_ref[...], b_ref[...])
        o_ref[...] = out
        if emit_bf16:
            ob_ref[...] = out.astype(BF16)


def _ffn_ln(x, wg, wu, wd, g, b, *, tm, th, emit_bf16, name, side_cast=()):
    t, d = x.shape
    hidden = wg.shape[1]
    assert t % tm == 0 and hidden % th == 0
    n_i, n_j = t // tm, hidden // th
    row = pl.BlockSpec((tm, d), lambda i, j: (i, 0))
    vec = pl.BlockSpec((1, d), lambda i, j: (0, 0))
    out_shape = [jax.ShapeDtypeStruct((t, d), F32)]
    out_specs = [row]
    if emit_bf16:
        out_shape.append(jax.ShapeDtypeStruct((t, d), BF16))
        out_specs.append(row)
    block_bytes = (2 * tm * d * 4 + 2 * 3 * d * th * 2 + 2 * tm * d * 4 + (2 * tm * d * 2 if emit_bf16 else 0)
                   + tm * d * 2 + tm * d * 4)
    side_specs = []
    for w in side_cast:
        r, c = w.shape
        if r % n_i == 0 and c % n_j == 0 and (c // n_j) % V7X_LANES == 0:
            spec = pl.BlockSpec((r // n_i, c // n_j), lambda i, j: (i, j))
        else:
            assert r % n_j == 0 and c % n_i == 0 and (c // n_i) % V7X_LANES == 0
            spec = pl.BlockSpec((r // n_j, c // n_i), lambda i, j: (j, i))
        side_specs.append(spec)
        out_shape.append(jax.ShapeDtypeStruct((r, c), BF16))
        block_bytes += 2 * (r * c // (n_i * n_j)) * (4 + 2)
    return pl.pallas_call(
        functools.partial(_ffn_ln_kernel, n_hidden_tiles=n_j, emit_bf16=emit_bf16, n_side=len(side_cast)),
        grid=(n_i, n_j),
        in_specs=[row,
                  pl.BlockSpec((d, th), lambda i, j: (0, j)),
                  pl.BlockSpec((d, th), lambda i, j: (0, j)),
                  pl.BlockSpec((th, d), lambda i, j: (j, 0)),
                  vec, vec] + side_specs,
        out_specs=out_specs + side_specs,
        out_shape=out_shape,
        scratch_shapes=[pltpu.VMEM((tm, d), BF16), pltpu.VMEM((tm, d), F32)],
        compiler_params=pltpu.CompilerParams(
            dimension_semantics=("parallel", "arbitrary"), vmem_limit_bytes=_vmem_limit(block_bytes)),
        name=name,
    )(x, wg, wu, wd, g, b, *side_cast)


def _in_proj_kernel(a_ref, w_ref, o_ref, wb_ref, *, apply_silu):
    @pl.when(pl.program_id(1) == 0)
    def _():
        _cast_weight(wb_ref, w_ref)

    raw = _dot_nt(a_ref[...], wb_ref[...])
    o_ref[...] = _silu(raw) if apply_silu else raw


def _in_proj(a, w_t, *, row0, n, tm, tn, weight_buffers, name, apply_silu=False):
    m, k = a.shape
    assert m % tm == 0 and n % tn == 0 and row0 % V7X_SUBLANES == 0
    w_spec = pl.BlockSpec((pl.Element(tn), pl.Element(k)),
                          lambda j, i: (pl.multiple_of(row0 + j * tn, V7X_SUBLANES), 0),
                          pipeline_mode=pl.Buffered(weight_buffers))
    block_bytes = 2 * (tm * k * 2 + tm * tn * 4) + weight_buffers * tn * k * 4 + tn * k * 2
    return pl.pallas_call(
        functools.partial(_in_proj_kernel, apply_silu=apply_silu),
        grid=(n // tn, m // tm),
        in_specs=[pl.BlockSpec((tm, k), lambda j, i: (i, 0)), w_spec],
        out_specs=pl.BlockSpec((tm, tn), lambda j, i: (i, j)),
        out_shape=jax.ShapeDtypeStruct((m, n), F32),
        scratch_shapes=[pltpu.VMEM((tn, k), BF16)],
        compiler_params=pltpu.CompilerParams(
            dimension_semantics=("parallel", "arbitrary"), vmem_limit_bytes=_vmem_limit(block_bytes)),
        name=name,
    )(a, w_t)


def _dt_proj_kernel(a_ref, w_ref, o_ref):
    o_ref[...] = _dot_nt(w_ref[...].astype(BF16), a_ref[...])


def _dt_proj(a, w_t, *, row0, n_heads, tm):
    m, k = a.shape
    assert row0 % n_heads == 0 and n_heads % V7X_SUBLANES == 0 and m % tm == 0
    block_bytes = 2 * (tm * k * 2 + n_heads * k * 4 + n_heads * tm * 4)
    return pl.pallas_call(
        _dt_proj_kernel,
        grid=(m // tm,),
        in_specs=[pl.BlockSpec((tm, k), lambda i: (i, 0)),
                  pl.BlockSpec((n_heads, k), lambda i: (row0 // n_heads, 0))],
        out_specs=pl.BlockSpec((n_heads, tm), lambda i: (0, i)),
        out_shape=jax.ShapeDtypeStruct((n_heads, m), F32),
        compiler_params=pltpu.CompilerParams(
            dimension_semantics=("parallel",), vmem_limit_bytes=_vmem_limit(block_bytes)),
        name="dt_proj",
    )(a, w_t)


def _ssd_kernel(x_ref, b_ref, c_ref, z_ref, dt_ref,
                cwx_ref, cwb_ref, cwc_ref, cbx_ref, cbb_ref, cbc_ref,
                bias_ref, alog_ref, dskip_ref, ng_ref,
                y_ref, xpad_ref, bpad_ref, cpad_ref, xs_ref, bm_ref, cm_ref, state_ref):
    q = SSD_CHUNK
    tail = V7X_SUBLANES
    rows = x_ref.shape[0]
    gw = x_ref.shape[1]
    heads = gw // SSD_HEAD_DIM
    pairs = heads // 2
    pw = 2 * SSD_HEAD_DIM

    @pl.when(pl.program_id(1) == 0)
    def _():
        xpad_ref[0:tail, :] = jnp.zeros((tail, xpad_ref.shape[1]), F32)
        bpad_ref[0:tail, :] = jnp.zeros((tail, bpad_ref.shape[1]), F32)
        cpad_ref[0:tail, :] = jnp.zeros((tail, cpad_ref.shape[1]), F32)
        state_ref[...] = jnp.zeros_like(state_ref)

    def conv_silu(u_ref, pad_ref, w_ref, cbias_ref, dst_ref):
        pad_ref[tail:tail + rows, :] = u_ref[...]
        acc = cbias_ref[...]
        for k in range(SSD_CONV_WIDTH):
            start = tail - (SSD_CONV_WIDTH - 1) + k
            acc = acc + w_ref[k:k + 1, :] * pad_ref[start:start + rows, :]
        pad_ref[0:tail, :] = pad_ref[rows:rows + tail, :]
        dst_ref[...] = _silu(acc).astype(dst_ref.dtype)

    conv_silu(x_ref, xpad_ref, cwx_ref, cbx_ref, xs_ref)
    conv_silu(b_ref, bpad_ref, cwb_ref, cbb_ref, bm_ref)
    conv_silu(c_ref, cpad_ref, cwc_ref, cbc_ref, cm_ref)

    dt_all = jax.nn.softplus(dt_ref[...] + bias_ref[...])
    adt_all = dt_all * (-jnp.exp(alog_ref[...]))

    ri = lax.broadcasted_iota(jnp.int32, (q, q), 0)
    ci = lax.broadcasted_iota(jnp.int32, (q, q), 1)
    causal = ci <= ri
    upper = (ri <= ci).astype(F32)
    lo = lax.broadcasted_iota(jnp.int32, (q, pw), 1) < SSD_HEAD_DIM
    lo_row = lax.broadcasted_iota(jnp.int32, (1, pw), 1) < SSD_HEAD_DIM

    staged = []
    for ck in range(rows // q):
        r = slice(ck * q, (ck + 1) * q)
        dt_r = dt_all[:, r]
        cs_r = _dot_f32(adt_all[:, r], upper)

        def over_lanes(row0, row1):
            stacked = jnp.concatenate([jnp.broadcast_to(row0, (SSD_HEAD_DIM, q)),
                                       jnp.broadcast_to(row1, (SSD_HEAD_DIM, q))], axis=0)
            return stacked.T

        cs_l = [over_lanes(cs_r[h:h + 1, :], cs_r[h:h + 1, :]) for h in range(heads)]

        bm_b = bm_ref[r, :]
        cm_b = cm_ref[r, :]
        cb = _dot_nt(cm_b, bm_b)

        y_parts, din_parts, xdd_parts, cd_parts = [], [], [], []
        for p in range(pairs):
            h0, h1 = 2 * p, 2 * p + 1
            sl = slice(p * pw, (p + 1) * pw)
            xp = xs_ref[r, sl]
            xd = xp * over_lanes(dt_r[h0:h0 + 1, :], dt_r[h1:h1 + 1, :])
            xd_b = xd.astype(BF16)
            y_diag = []
            for h in (h0, h1):
                seg = jnp.where(causal, jnp.exp(cs_l[h] - cs_r[h:h + 1, :]), 0.0)
                y_diag.append(_dot((cb * seg).astype(BF16), xd_b))
            cs_p = jnp.where(lo, cs_l[h0], cs_l[h1])
            cs_end = jnp.where(lo_row, cs_l[h0][q - 1:q, :], cs_l[h1][q - 1:q, :])
            y_parts.append(jnp.where(lo, y_diag[0], y_diag[1]) + dskip_ref[:, sl] * xp)
            din_parts.append(jnp.exp(cs_p))
            xdd_parts.append((xd * jnp.exp(cs_end - cs_p)).astype(BF16))
            cd_parts.append(jnp.exp(cs_end))
        staged.append((cm_b, bm_b, jnp.concatenate(y_parts, axis=1), jnp.concatenate(din_parts, axis=1),
                       jnp.concatenate(xdd_parts, axis=1), jnp.concatenate(cd_parts, axis=1)))

    for ck, (cm_b, bm_b, y_local, decay_in, xdd, chunk_decay) in enumerate(staged):
        r = slice(ck * q, (ck + 1) * q)
        state = state_ref[...]
        y = y_local + _dot(cm_b, state.astype(BF16)) * decay_in
        state_ref[...] = state * chunk_decay + _dot_tn(bm_b, xdd)
        y = y * z_ref[r, :]
        y = y * lax.rsqrt(jnp.mean(y * y, axis=-1, keepdims=True) + RMS_EPS) * ng_ref[...]
        y_ref[r, :] = y.astype(y_ref.dtype)


def _ssd(zs, xbc, dt_t, conv_w, conv_b, dt_bias, a_log, d_skip, norm_g, *, d_inner, rows):
    t = zs.shape[0]
    g = SSD_N_GROUPS
    gw = d_inner // g
    hg = gw // SSD_HEAD_DIM
    n = SSD_D_STATE
    assert t % rows == 0 and rows % SSD_CHUNK == 0 and d_inner % n == 0
    assert 2 * hg <= SSD_CHUNK and hg % V7X_SUBLANES == 0
    conv_b2 = conv_b.reshape(1, -1)
    per_head = lambda v: v.reshape(g * hg, 1)
    dskip_l = jnp.repeat(d_skip, SSD_HEAD_DIM).reshape(1, d_inner)
    ng = norm_g.reshape(1, d_inner)
    b_blk = d_inner // n
    c_blk = b_blk + g

    head_col = pl.BlockSpec((hg, 1), lambda gi, ci: (gi, 0))
    in_specs = [
        pl.BlockSpec((rows, gw), lambda gi, ci: (ci, gi)),
        pl.BlockSpec((rows, n), lambda gi, ci: (ci, b_blk + gi)),
        pl.BlockSpec((rows, n), lambda gi, ci: (ci, c_blk + gi)),
        pl.BlockSpec((rows, gw), lambda gi, ci: (ci, gi)),
        pl.BlockSpec((hg, rows), lambda gi, ci: (gi, ci)),
        pl.BlockSpec((SSD_CONV_WIDTH, gw), lambda gi, ci: (0, gi)),
        pl.BlockSpec((SSD_CONV_WIDTH, n), lambda gi, ci: (0, b_blk + gi)),
        pl.BlockSpec((SSD_CONV_WIDTH, n), lambda gi, ci: (0, c_blk + gi)),
        pl.BlockSpec((1, gw), lambda gi, ci: (0, gi)),
        pl.BlockSpec((1, n), lambda gi, ci: (0, b_blk + gi)),
        pl.BlockSpec((1, n), lambda gi, ci: (0, c_blk + gi)),
        head_col, head_col,
        pl.BlockSpec((1, gw), lambda gi, ci: (0, gi)),
        pl.BlockSpec((1, gw), lambda gi, ci: (0, gi)),
    ]
    pad_rows = rows + V7X_SUBLANES
    block_bytes = (2 * (2 * rows * gw * 4 + 2 * rows * n * 4 + rows * gw * 2) + pad_rows * (gw + 2 * n) * 4
                   + rows * gw * 4 + 2 * rows * n * 2 + n * gw * 4)
    return pl.pallas_call(
        _ssd_kernel,
        grid=(g, t // rows),
        in_specs=in_specs,
        out_specs=pl.BlockSpec((rows, gw), lambda gi, ci: (ci, gi)),
        out_shape=jax.ShapeDtypeStruct((t, d_inner), BF16),
        scratch_shapes=[pltpu.VMEM((pad_rows, gw), F32), pltpu.VMEM((pad_rows, n), F32),
                        pltpu.VMEM((pad_rows, n), F32), pltpu.VMEM((rows, gw), F32),
                        pltpu.VMEM((rows, n), BF16), pltpu.VMEM((rows, n), BF16),
                        pltpu.VMEM((n, gw), F32)],
        compiler_params=pltpu.CompilerParams(
            dimension_semantics=("parallel", "arbitrary"), vmem_limit_bytes=_vmem_limit(block_bytes)),
        name="ssd",
    )(xbc, xbc, xbc, zs, dt_t, conv_w, conv_w, conv_w, conv_b2, conv_b2, conv_b2,
      per_head(dt_bias), per_head(a_log), dskip_l, ng)


def _swa_kernel(q_ref, k_ref, v_ref, pos_ref, invf_ref, sink_ref, o_ref, kprev_ref, vprev_ref):
    w = ATTN_WINDOW
    dh = ATTN_HEAD_DIM
    half = dh // 2
    lanes = 2 * dh
    n_kv = k_ref.shape[1] // dh
    tiles_per_kv = ATTN_GROUP // 2
    nb = pl.program_id(0)

    @pl.when(nb == 0)
    def _():
        kprev_ref[...] = jnp.zeros_like(kprev_ref)
        vprev_ref[...] = jnp.zeros_like(vprev_ref)

    lane = lax.broadcasted_iota(jnp.int32, (w, lanes), 1)
    first_half = (lane % dh) < half
    lo2 = lax.broadcasted_iota(jnp.int32, (2 * w, lanes), 1) < dh

    ang = pos_ref[...].astype(F32) * invf_ref[...]
    cos = jnp.cos(ang)
    sin = jnp.sin(ang)
    sin = jnp.where(first_half, -sin, sin)
    scale = dh ** -0.5

    def rope(u, c, s):
        partner = jnp.where(first_half, pltpu.roll(u, lanes - half, 1), pltpu.roll(u, half, 1))
        return u * c + partner * s

    kj = lax.broadcasted_iota(jnp.int32, (2 * w, w), 0)
    qi = lax.broadcasted_iota(jnp.int32, (2 * w, w), 1)
    valid = (kj > qi) & (kj <= qi + w) & ((kj >= w) | (nb > 0))
    bias = jnp.where(valid, 0.0, -jnp.inf)

    cos_q, sin_q = cos * (scale * LOG2_E), sin * (scale * LOG2_E)
    q_tiles = [rope(q_ref[:, t * lanes:(t + 1) * lanes], cos_q, sin_q).astype(BF16)
               for t in range(q_ref.shape[1] // lanes)]

    units = []
    for pt in range(n_kv // 2):
        sl = slice(pt * lanes, (pt + 1) * lanes)
        k_cur = rope(k_ref[:, sl], cos, sin)
        v_cur = v_ref[:, sl]
        k_pair = jnp.concatenate([kprev_ref[:, sl], k_cur], axis=0)
        v_pair = jnp.concatenate([vprev_ref[:, sl], v_cur], axis=0)
        kprev_ref[:, sl] = k_cur
        vprev_ref[:, sl] = v_cur
        k_swap = pltpu.roll(k_pair, dh, 1)
        vt_pair = v_pair.T
        for side in range(2):
            hk = 2 * pt + side
            k_lo = jnp.where(lo2, k_pair if side == 0 else k_swap, 0.0).astype(BF16)
            k_hi = jnp.where(lo2, 0.0, k_swap if side == 0 else k_pair).astype(BF16)
            vt = vt_pair[side * dh:(side + 1) * dh, :].astype(BF16)
            units += [(hk, 0, k_lo, vt), (hk, 1, k_hi, vt)]

    scores = []
    for hk, parity, k_sel, vt in units:
        qs = jnp.concatenate(q_tiles[hk * tiles_per_kv:(hk + 1) * tiles_per_kv], axis=0)
        scores.append(_dot_nt(k_sel, qs))

    probs = []
    for (hk, parity, k_sel, vt), st in zip(units, scores):
        p_parts, den_parts = [], []
        for j in range(tiles_per_kv):
            head = hk * ATTN_GROUP + 2 * j + parity
            s = st[:, j * w:(j + 1) * w] + bias
            sink = sink_ref[0:1, head:head + 1] * LOG2_E
            m = jnp.maximum(jnp.max(s, axis=0, keepdims=True), sink)
            p = jnp.exp2(s - m)
            den_parts.append(jnp.sum(p, axis=0, keepdims=True) + jnp.exp2(sink - m))
            p_parts.append(p.astype(BF16))
        probs.append((jnp.concatenate(p_parts, axis=1), jnp.concatenate(den_parts, axis=1)))

    outs = {}
    for (hk, parity, k_sel, vt), (pt_all, den) in zip(units, probs):
        outs[hk, parity] = _dot(vt, pt_all) / den

    for hk in range(n_kv):
        for j in range(tiles_per_kv):
            qt = hk * tiles_per_kv + j
            tile_t = jnp.concatenate([outs[hk, 0][:, j * w:(j + 1) * w],
                                      outs[hk, 1][:, j * w:(j + 1) * w]], axis=0)
            o_ref[:, qt * lanes:(qt + 1) * lanes] = tile_t.T.astype(o_ref.dtype)


def _swa(proj, positions, sinks, *, q_col, q_width, k_col, v_col, kv_width):
    t = proj.shape[0]
    w = ATTN_WINDOW
    assert t % w == 0 and q_col % q_width == 0 and k_col % kv_width == 0 and v_col % kv_width == 0
    half = ATTN_HEAD_DIM // 2
    inv_freq = ROPE_THETA ** (-jnp.arange(half, dtype=F32) * 2.0 / ATTN_HEAD_DIM)
    inv_freq = jnp.tile(inv_freq, 2 * V7X_LANES // ATTN_HEAD_DIM).reshape(1, V7X_LANES)
    block_bytes = 2 * (w * q_width * 4 + 2 * w * kv_width * 4 + w * q_width * 2) + 2 * w * kv_width * 4
    return pl.pallas_call(
        _swa_kernel,
        grid=(t // w,),
        in_specs=[
            pl.BlockSpec((w, q_width), lambda i: (i, q_col // q_width)),
            pl.BlockSpec((w, kv_width), lambda i: (i, k_col // kv_width)),
            pl.BlockSpec((w, kv_width), lambda i: (i, v_col // kv_width)),
            pl.BlockSpec((w, 1), lambda i: (i, 0)),
            pl.BlockSpec((1, V7X_LANES), lambda i: (0, 0)),
            pl.BlockSpec((1, sinks.shape[0]), lambda i: (0, 0)),
        ],
        out_specs=pl.BlockSpec((w, q_width), lambda i: (i, 0)),
        out_shape=jax.ShapeDtypeStruct((t, q_width), BF16),
        scratch_shapes=[pltpu.VMEM((w, kv_width), F32), pltpu.VMEM((w, kv_width), F32)],
        compiler_params=pltpu.CompilerParams(
            dimension_semantics=("arbitrary",), vmem_limit_bytes=_vmem_limit(block_bytes)),
        name="swa",
    )(proj, proj, proj, positions.reshape(t, 1), inv_freq, sinks.reshape(1, -1))


def _merge_kernel(ys_ref, ya_ref, ws_ref, wa_ref, gs_ref, ga_ref, o_ref, wsb_ref, wab_ref):
    @pl.when(pl.program_id(1) == 0)
    def _():
        _cast_weight(wsb_ref, ws_ref)
        _cast_weight(wab_ref, wa_ref)

    y_s = _dot(ys_ref[...], wsb_ref[...])
    y_a = _dot(ya_ref[...], wab_ref[...])
    merged = jax.nn.sigmoid(gs_ref[...]) * y_s + jax.nn.sigmoid(ga_ref[...]) * y_a
    o_ref[...] = merged.astype(o_ref.dtype)


def _merge(y_ssd, y_attn, w_ssd_o, w_attn_o, proj, *, gs_col, ga_col, tm, tn):
    t, ks = y_ssd.shape
    ka = y_attn.shape[1]
    d = w_ssd_o.shape[1]
    assert t % tm == 0 and d % tn == 0 and gs_col % tn == 0 and ga_col % tn == 0
    block_bytes = (2 * (tm * ks * 2 + tm * ka * 2 + ks * tn * 4 + ka * tn * 4 + 2 * tm * tn * 4 + tm * tn * 2)
                   + (ks + ka) * tn * 2)
    return pl.pallas_call(
        _merge_kernel,
        grid=(d // tn, t // tm),
        in_specs=[pl.BlockSpec((tm, ks), lambda j, i: (i, 0)),
                  pl.BlockSpec((tm, ka), lambda j, i: (i, 0)),
                  pl.BlockSpec((ks, tn), lambda j, i: (0, j)),
                  pl.BlockSpec((ka, tn), lambda j, i: (0, j)),
                  pl.BlockSpec((tm, tn), lambda j, i: (i, gs_col // tn + j)),
                  pl.BlockSpec((tm, tn), lambda j, i: (i, ga_col // tn + j))],
        out_specs=pl.BlockSpec((tm, tn), lambda j, i: (i, j)),
        out_shape=jax.ShapeDtypeStruct((t, d), BF16),
        scratch_shapes=[pltpu.VMEM((ks, tn), BF16), pltpu.VMEM((ka, tn), BF16)],
        compiler_params=pltpu.CompilerParams(
            dimension_semantics=("parallel", "arbitrary"), vmem_limit_bytes=_vmem_limit(block_bytes)),
        name="merge",
    )(y_ssd, y_attn, w_ssd_o, w_attn_o, proj, proj)


def _proj_ln_kernel(m_ref, w_ref, h_ref, g_ref, b_ref, o_ref, wb_ref):
    @pl.when(pl.program_id(0) == 0)
    def _():
        _cast_weight(wb_ref, w_ref)

    y = DEEPNORM_ALPHA * h_ref[...] + _dot(m_ref[...], wb_ref[...])
    o_ref[...] = _layer_norm(y, g_ref[...], b_ref[...])


def _proj_ln(merged, w_out, h, g, b, *, tm):
    t, d = h.shape
    k = merged.shape[1]
    assert t % tm == 0
    vec = pl.BlockSpec((1, d), lambda i: (0, 0))
    block_bytes = 2 * (tm * k * 2 + 2 * tm * d * 4) + k * d * 4 + k * d * 2
    return pl.pallas_call(
        _proj_ln_kernel,
        grid=(t // tm,),
        in_specs=[pl.BlockSpec((tm, k), lambda i: (i, 0)),
                  pl.BlockSpec((k, d), lambda i: (0, 0), pipeline_mode=pl.Buffered(1)),
                  pl.BlockSpec((tm, d), lambda i: (i, 0)),
                  vec, vec],
        out_specs=pl.BlockSpec((tm, d), lambda i: (i, 0)),
        out_shape=jax.ShapeDtypeStruct((t, d), F32),
        scratch_shapes=[pltpu.VMEM((k, d), BF16)],
        compiler_params=pltpu.CompilerParams(
            dimension_semantics=("arbitrary",), vmem_limit_bytes=_vmem_limit(block_bytes)),
        name="proj_ln",
    )(merged, w_out, h, g, b)


def _layer(x, positions, ffn1_w_gate, ffn1_w_up, ffn1_w_down, ln1_g, ln1_b,
           w_in, conv_w, conv_b, dt_bias, a_log, d_skip, ssd_norm_g, w_ssd_o,
           attn_sinks, w_attn_o, w_out, ln2_g, ln2_b,
           ffn2_w_gate, ffn2_w_up, ffn2_w_down, ln3_g, ln3_b):
    t, d = x.shape
    d_inner = w_ssd_o.shape[0]
    n_ssd_heads = dt_bias.shape[0]
    q_width = w_attn_o.shape[0]
    kv_width = q_width // ATTN_GROUP
    xbc_width = d_inner + 2 * SSD_N_GROUPS * SSD_D_STATE
    vec = lambda v: v.reshape(1, -1)
    tm_ffn = min(512, t)
    tm_mm = min(1024, t)

    sizes = (d_inner, xbc_width, n_ssd_heads, q_width, kv_width, kv_width, d, d)
    starts = [0]
    for s in sizes:
        starts.append(starts[-1] + s)
    dt_lo, dt_hi, total = starts[2], starts[3], starts[-1]

    h1, h1b, wg2, wu2, wd2 = _ffn_ln(
        x, ffn1_w_gate.astype(BF16), ffn1_w_up.astype(BF16), ffn1_w_down.astype(BF16), vec(ln1_g), vec(ln1_b),
        tm=tm_ffn, th=512, emit_bf16=True, name="ffn1_ln", side_cast=(ffn2_w_gate, ffn2_w_up, ffn2_w_down))

    w_in_t = w_in.T
    n_b = total - dt_hi
    zs = _in_proj(h1b, w_in_t, row0=0, n=d_inner, tm=tm_mm, tn=1024, weight_buffers=2, name="in_proj_z",
                  apply_silu=True)
    xbc = _in_proj(h1b, w_in_t, row0=d_inner, n=xbc_width, tm=tm_mm, tn=1024, weight_buffers=2,
                   name="in_proj_xbc")
    proj_b = _in_proj(h1b, w_in_t, row0=dt_hi, n=n_b, tm=tm_mm, tn=n_b // 4, weight_buffers=1, name="in_proj_b")
    dt_t = _dt_proj(h1b, w_in_t, row0=dt_lo, n_heads=n_ssd_heads, tm=tm_mm)

    y_ssd = _ssd(zs, xbc, dt_t, conv_w, conv_b, dt_bias, a_log, d_skip, ssd_norm_g,
                 d_inner=d_inner, rows=min(SSD_ROWS_PER_STEP, t))
    y_attn = _swa(proj_b, positions, attn_sinks, q_col=0, q_width=q_width,
                  k_col=q_width, v_col=q_width + kv_width, kv_width=kv_width)

    merged = _merge(y_ssd, y_attn, w_ssd_o, w_attn_o, proj_b,
                    gs_col=q_width + 2 * kv_width, ga_col=q_width + 2 * kv_width + d, tm=tm_ffn, tn=512)
    h2 = _proj_ln(merged, w_out, h1, vec(ln2_g), vec(ln2_b), tm=tm_ffn)

    (out,) = _ffn_ln(h2, wg2, wu2, wd2, vec(ln3_g), vec(ln3_b), tm=tm_ffn, th=512, emit_bf16=False, name="ffn2_ln")
    return out


def kernel(x, positions, ffn1_w_gate, ffn1_w_up, ffn1_w_down, ln1_g, ln1_b, w_in, conv_w, conv_b, dt_bias, a_log, d_skip, ssd_norm_g, w_ssd_o, attn_sinks, w_attn_o, w_out, ln2_g, ln2_b, ffn2_w_gate, ffn2_w_up, ffn2_w_down, ln3_g, ln3_b):
    batch, depth = x.shape[0], ffn1_w_gate.shape[0]
    assert depth == DEPTH
    outs = []
    for bi in range(batch):
        h = x[bi]
        for l in range(depth):
            h = _layer(h, positions[bi], ffn1_w_gate[l], ffn1_w_up[l], ffn1_w_down[l], ln1_g[l], ln1_b[l],
                       w_in[l], conv_w[l], conv_b[l], dt_bias[l], a_log[l], d_skip[l], ssd_norm_g[l], w_ssd_o[l],
                       attn_sinks[l], w_attn_o[l], w_out[l], ln2_g[l], ln2_b[l],
                       ffn2_w_gate[l], ffn2_w_up[l], ffn2_w_down[l], ln3_g[l], ln3_b[l])
        outs.append(h)
    return jnp.stack(outs, axis=0)
```

```python
import functools
import math

import jax
import jax.numpy as jnp
from jax import lax
from jax.experimental import pallas as pl
from jax.experimental.pallas import tpu as pltpu

F32 = jnp.float32
BF16 = jnp.bfloat16

V7X_LANES = 128
V7X_SUBLANES = 8
V7X_VMEM_BYTES = 64 * 1024 * 1024

SSD_HEAD_DIM = 64
SSD_N_GROUPS = 8
SSD_D_STATE = 128
SSD_CONV_WIDTH = 4
SSD_CHUNK = 128
ATTN_HEAD_DIM = 64
ATTN_GROUP = 8
ATTN_WINDOW = 128
ROPE_THETA = 10000.0
DEPTH = 1
DEEPNORM_ALPHA = (2 * DEPTH) ** 0.25
LN_EPS = 1e-5
RMS_EPS = 1e-5
LOG2_E = math.log2(math.e)

CAST_ROWS = 256
SSD_ROWS_PER_STEP = 8 * SSD_CHUNK


def _vmem_limit(block_bytes):
    return int(min(block_bytes * 5 // 4 + (8 << 20), V7X_VMEM_BYTES - (6 << 20)))


def _dot(a, b):
    return jnp.dot(a, b, preferred_element_type=F32)


def _dot_nt(a, b):
    return lax.dot_general(a, b, (((1,), (1,)), ((), ())), preferred_element_type=F32)


def _dot_tn(a, b):
    return lax.dot_general(a, b, (((0,), (0,)), ((), ())), preferred_element_type=F32)


def _dot_f32(a, b):
    return jnp.dot(a, b, preferred_element_type=F32, precision=lax.Precision.HIGHEST)


def _layer_norm(y, g, b):
    mu = jnp.mean(y, axis=-1, keepdims=True)
    yc = y - mu
    var = jnp.mean(yc * yc, axis=-1, keepdims=True)
    return yc * lax.rsqrt(var + LN_EPS) * g + b


def _silu(x):
    return x * jax.nn.sigmoid(x)


def _cast_weight(dst_ref, w_ref):
    k = dst_ref.shape[0]
    for r0 in range(0, k, CAST_ROWS):
        rows = slice(r0, min(r0 + CAST_ROWS, k))
        dst_ref[rows, :] = w_ref[rows, :].astype(BF16)


def _ffn_ln_kernel(x_ref, wg_ref, wu_ref, wd_ref, g_ref, b_ref, *refs, n_hidden_tiles, emit_bf16, n_side):
    side_in, refs = refs[:n_side], refs[n_side:]
    if emit_bf16:
        o_ref, ob_ref = refs[:2]
        refs = refs[2:]
    else:
        o_ref = refs[0]
        refs = refs[1:]
    side_out, (xb_ref, acc_ref) = refs[:n_side], refs[n_side:]
    j = pl.program_id(1)

    for src_ref, dst_ref in zip(side_in, side_out):
        dst_ref[...] = src_ref[...].astype(BF16)

    @pl.when(j == 0)
    def _():
        xb_ref[...] = x_ref[...].astype(BF16)
        acc_ref[...] = jnp.zeros_like(acc_ref)

    xb = xb_ref[...]
    gate = _dot(xb, wg_ref[...])
    up = _dot(xb, wu_ref[...])
    act = (_silu(gate) * up).astype(BF16)
    acc_ref[...] += _dot(act, wd_ref[...])

    @pl.when(j == n_hidden_tiles - 1)
    def _():
        y = DEEPNORM_ALPHA * x_ref[...] + 0.5 * acc_ref[...]
        out = _layer_norm(y, g_ref[...], b_ref[...])
        o_ref[...] = out
        if emit_bf16:
            ob_ref[...] = out.astype(BF16)


def _ffn_ln(x, wg, wu, wd, g, b, *, tm, th, emit_bf16, name, side_cast=()):
    t, d = x.shape
    hidden = wg.shape[1]
    assert t % tm == 0 and hidden % th == 0
    n_i, n_j = t // tm, hidden // th
    row = pl.BlockSpec((tm, d), lambda i, j: (i, 0))
    vec = pl.BlockSpec((1, d), lambda i, j: (0, 0))
    out_shape = [jax.ShapeDtypeStruct((t, d), F32)]
    out_specs = [row]
    if emit_bf16:
        out_shape.append(jax.ShapeDtypeStruct((t, d), BF16))
        out_specs.append(row)
    block_bytes = (2 * tm * d * 4 + 2 * 3 * d * th * 2 + 2 * tm * d * 4 + (2 * tm * d * 2 if emit_bf16 else 0)
                   + tm * d * 2 + tm * d * 4)
    side_specs = []
    for w in side_cast:
        r, c = w.shape
        if r % n_i == 0 and c % n_j == 0 and (c // n_j) % V7X_LANES == 0:
            spec = pl.BlockSpec((r // n_i, c // n_j), lambda i, j: (i, j))
        else:
            assert r % n_j == 0 and c % n_i == 0 and (c // n_i) % V7X_LANES == 0
            spec = pl.BlockSpec((r // n_j, c // n_i), lambda i, j: (j, i))
        side_specs.append(spec)
        out_shape.append(jax.ShapeDtypeStruct((r, c), BF16))
        block_bytes += 2 * (r * c // (n_i * n_j)) * (4 + 2)
    return pl.pallas_call(
        functools.partial(_ffn_ln_kernel, n_hidden_tiles=n_j, emit_bf16=emit_bf16, n_side=len(side_cast)),
        grid=(n_i, n_j),
        in_specs=[row,
                  pl.BlockSpec((d, th), lambda i, j: (0, j)),
                  pl.BlockSpec((d, th), lambda i, j: (0, j)),
                  pl.BlockSpec((th, d), lambda i, j: (j, 0)),
                  vec, vec] + side_specs,
        out_specs=out_specs + side_specs,
        out_shape=out_shape,
        scratch_shapes=[pltpu.VMEM((tm, d), BF16), pltpu.VMEM((tm, d), F32)],
        compiler_params=pltpu.CompilerParams(
            dimension_semantics=("parallel", "arbitrary"), vmem_limit_bytes=_vmem_limit(block_bytes)),
        name=name,
    )(x, wg, wu, wd, g, b, *side_cast)


def _in_proj_kernel(a_ref, w_ref, o_ref, wb_ref, *, apply_silu):
    @pl.when(pl.program_id(1) == 0)
    def _():
        _cast_weight(wb_ref, w_ref)

    raw = _dot_nt(a_ref[...], wb_ref[...])
    o_ref[...] = _silu(raw) if apply_silu else raw


def _in_proj(a, w_t, *, row0, n, tm, tn, weight_buffers, name, apply_silu=False):
    m, k = a.shape
    assert m % tm == 0 and n % tn == 0 and row0 % V7X_SUBLANES == 0
    w_spec = pl.BlockSpec((pl.Element(tn), pl.Element(k)),
                          lambda j, i: (pl.multiple_of(row0 + j * tn, V7X_SUBLANES), 0),
                          pipeline_mode=pl.Buffered(weight_buffers))
    block_bytes = 2 * (tm * k * 2 + tm * tn * 4) + weight_buffers * tn * k * 4 + tn * k * 2
    return pl.pallas_call(
        functools.partial(_in_proj_kernel, apply_silu=apply_silu),
        grid=(n // tn, m // tm),
        in_specs=[pl.BlockSpec((tm, k), lambda j, i: (i, 0)), w_spec],
        out_specs=pl.BlockSpec((tm, tn), lambda j, i: (i, j)),
        out_shape=jax.ShapeDtypeStruct((m, n), F32),
        scratch_shapes=[pltpu.VMEM((tn, k), BF16)],
        compiler_params=pltpu.CompilerParams(
            dimension_semantics=("parallel", "arbitrary"), vmem_limit_bytes=_vmem_limit(block_bytes)),
        name=name,
    )(a, w_t)


def _dt_proj_kernel(a_ref, w_ref, o_ref):
    o_ref[...] = _dot_nt(w_ref[...].astype(BF16), a_ref[...])


def _dt_proj(a, w_t, *, row0, n_heads, tm):
    m, k = a.shape
    assert row0 % n_heads == 0 and n_heads % V7X_SUBLANES == 0 and m % tm == 0
    block_bytes = 2 * (tm * k * 2 + n_heads * k * 4 + n_heads * tm * 4)
    return pl.pallas_call(
        _dt_proj_kernel,
        grid=(m // tm,),
        in_specs=[pl.BlockSpec((tm, k), lambda i: (i, 0)),
                  pl.BlockSpec((n_heads, k), lambda i: (row0 // n_heads, 0))],
        out_specs=pl.BlockSpec((n_heads, tm), lambda i: (0, i)),
        out_shape=jax.ShapeDtypeStruct((n_heads, m), F32),
        compiler_params=pltpu.CompilerParams(
            dimension_semantics=("parallel",), vmem_limit_bytes=_vmem_limit(block_bytes)),
        name="dt_proj",
    )(a, w_t)


def _ssd_kernel(x_ref, b_ref, c_ref, z_ref, dt_ref,
                cwx_ref, cwb_ref, cwc_ref, cbx_ref, cbb_ref, cbc_ref,
                bias_ref, alog_ref, dskip_ref, ng_ref,
                y_ref, xpad_ref, bpad_ref, cpad_ref, xs_ref, bm_ref, cm_ref, state_ref):
    q = SSD_CHUNK
    tail = V7X_SUBLANES
    rows = x_ref.shape[0]
    gw = x_ref.shape[1]
    heads = gw // SSD_HEAD_DIM
    pairs = heads // 2
    pw = 2 * SSD_HEAD_DIM

    @pl.when(pl.program_id(1) == 0)
    def _():
        xpad_ref[0:tail, :] = jnp.zeros((tail, xpad_ref.shape[1]), F32)
        bpad_ref[0:tail, :] = jnp.zeros((tail, bpad_ref.shape[1]), F32)
        cpad_ref[0:tail, :] = jnp.zeros((tail, cpad_ref.shape[1]), F32)
        state_ref[...] = jnp.zeros_like(state_ref)

    def conv_silu(u_ref, pad_ref, w_ref, cbias_ref, dst_ref):
        pad_ref[tail:tail + rows, :] = u_ref[...]
        acc = cbias_ref[...]
        for k in range(SSD_CONV_WIDTH):
            start = tail - (SSD_CONV_WIDTH - 1) + k
            acc = acc + w_ref[k:k + 1, :] * pad_ref[start:start + rows, :]
        pad_ref[0:tail, :] = pad_ref[rows:rows + tail, :]
        dst_ref[...] = _silu(acc).astype(dst_ref.dtype)

    conv_silu(x_ref, xpad_ref, cwx_ref, cbx_ref, xs_ref)
    conv_silu(b_ref, bpad_ref, cwb_ref, cbb_ref, bm_ref)
    conv_silu(c_ref, cpad_ref, cwc_ref, cbc_ref, cm_ref)

    dt_all = jax.nn.softplus(dt_ref[...] + bias_ref[...])
    adt_all = dt_all * (-jnp.exp(alog_ref[...]) * LOG2_E)

    ri = lax.broadcasted_iota(jnp.int32, (q, q), 0)
    ci = lax.broadcasted_iota(jnp.int32, (q, q), 1)
    causal = ci <= ri
    upper = (ri <= ci).astype(F32)
    lo = lax.broadcasted_iota(jnp.int32, (q, pw), 1) < SSD_HEAD_DIM
    lo_row = lax.broadcasted_iota(jnp.int32, (1, pw), 1) < SSD_HEAD_DIM

    staged = []
    for ck in range(rows // q):
        r = slice(ck * q, (ck + 1) * q)
        dt_r = dt_all[:, r]
        cs_r = _dot_f32(adt_all[:, r], upper)

        def over_lanes(row0, row1):
            stacked = jnp.concatenate([jnp.broadcast_to(row0, (SSD_HEAD_DIM, q)),
                                       jnp.broadcast_to(row1, (SSD_HEAD_DIM, q))], axis=0)
            return stacked.T

        cs_l = [over_lanes(cs_r[h:h + 1, :], cs_r[h:h + 1, :]) for h in range(heads)]

        bm_b = bm_ref[r, :]
        cm_b = cm_ref[r, :]
        cb = _dot_nt(cm_b, bm_b)

        y_parts, din_parts, xdd_parts, cd_parts = [], [], [], []
        for p in range(pairs):
            h0, h1 = 2 * p, 2 * p + 1
            sl = slice(p * pw, (p + 1) * pw)
            xp = xs_ref[r, sl]
            xd = xp * over_lanes(dt_r[h0:h0 + 1, :], dt_r[h1:h1 + 1, :])
            xd_b = xd.astype(BF16)
            y_diag = []
            for h in (h0, h1):
                seg = jnp.where(causal, jnp.exp2(cs_l[h] - cs_r[h:h + 1, :]), 0.0)
                y_diag.append(_dot((cb * seg).astype(BF16), xd_b))
            cs_p = jnp.where(lo, cs_l[h0], cs_l[h1])
            cs_end = jnp.where(lo_row, cs_l[h0][q - 1:q, :], cs_l[h1][q - 1:q, :])
            y_parts.append(jnp.where(lo, y_diag[0], y_diag[1]) + dskip_ref[:, sl] * xp)
            din_parts.append(jnp.exp2(cs_p))
            xdd_parts.append((xd * jnp.exp2(cs_end - cs_p)).astype(BF16))
            cd_parts.append(jnp.exp2(cs_end))
        staged.append((cm_b, bm_b, jnp.concatenate(y_parts, axis=1), jnp.concatenate(din_parts, axis=1),
                       jnp.concatenate(xdd_parts, axis=1), jnp.concatenate(cd_parts, axis=1)))

    for ck, (cm_b, bm_b, y_local, decay_in, xdd, chunk_decay) in enumerate(staged):
        r = slice(ck * q, (ck + 1) * q)
        state = state_ref[...]
        y = y_local + _dot(cm_b, state.astype(BF16)) * decay_in
        state_ref[...] = state * chunk_decay + _dot_tn(bm_b, xdd)
        y = y * z_ref[r, :]
        y = y * lax.rsqrt(jnp.mean(y * y, axis=-1, keepdims=True) + RMS_EPS) * ng_ref[...]
        y_ref[r, :] = y.astype(y_ref.dtype)


def _ssd(zs, xbc, dt_t, conv_w, conv_b, dt_bias, a_log, d_skip, norm_g, *, d_inner, rows):
    t = zs.shape[0]
    g = SSD_N_GROUPS
    gw = d_inner // g
    hg = gw // SSD_HEAD_DIM
    n = SSD_D_STATE
    assert t % rows == 0 and rows % SSD_CHUNK == 0 and d_inner % n == 0
    assert 2 * hg <= SSD_CHUNK and hg % V7X_SUBLANES == 0
    conv_b2 = conv_b.reshape(1, -1)
    per_head = lambda v: v.reshape(g * hg, 1)
    dskip_l = jnp.repeat(d_skip, SSD_HEAD_DIM).reshape(1, d_inner)
    ng = norm_g.reshape(1, d_inner)
    b_blk = d_inner // n
    c_blk = b_blk + g

    head_col = pl.BlockSpec((hg, 1), lambda gi, ci: (gi, 0))
    in_specs = [
        pl.BlockSpec((rows, gw), lambda gi, ci: (ci, gi)),
        pl.BlockSpec((rows, n), lambda gi, ci: (ci, b_blk + gi)),
        pl.BlockSpec((rows, n), lambda gi, ci: (ci, c_blk + gi)),
        pl.BlockSpec((rows, gw), lambda gi, ci: (ci, gi)),
        pl.BlockSpec((hg, rows), lambda gi, ci: (gi, ci)),
        pl.BlockSpec((SSD_CONV_WIDTH, gw), lambda gi, ci: (0, gi)),
        pl.BlockSpec((SSD_CONV_WIDTH, n), lambda gi, ci: (0, b_blk + gi)),
        pl.BlockSpec((SSD_CONV_WIDTH, n), lambda gi, ci: (0, c_blk + gi)),
        pl.BlockSpec((1, gw), lambda gi, ci: (0, gi)),
        pl.BlockSpec((1, n), lambda gi, ci: (0, b_blk + gi)),
        pl.BlockSpec((1, n), lambda gi, ci: (0, c_blk + gi)),
        head_col, head_col,
        pl.BlockSpec((1, gw), lambda gi, ci: (0, gi)),
        pl.BlockSpec((1, gw), lambda gi, ci: (0, gi)),
    ]
    pad_rows = rows + V7X_SUBLANES
    block_bytes = (2 * (2 * rows * gw * 4 + 2 * rows * n * 4 + rows * gw * 2) + pad_rows * (gw + 2 * n) * 4
                   + rows * gw * 4 + 2 * rows * n * 2 + n * gw * 4)
    return pl.pallas_call(
        _ssd_kernel,
        grid=(g, t // rows),
        in_specs=in_specs,
        out_specs=pl.BlockSpec((rows, gw), lambda gi, ci: (ci, gi)),
        out_shape=jax.ShapeDtypeStruct((t, d_inner), BF16),
        scratch_shapes=[pltpu.VMEM((pad_rows, gw), F32), pltpu.VMEM((pad_rows, n), F32),
                        pltpu.VMEM((pad_rows, n), F32), pltpu.VMEM((rows, gw), F32),
                        pltpu.VMEM((rows, n), BF16), pltpu.VMEM((rows, n), BF16),
                        pltpu.VMEM((n, gw), F32)],
        compiler_params=pltpu.CompilerParams(
            dimension_semantics=("parallel", "arbitrary"), vmem_limit_bytes=_vmem_limit(block_bytes)),
        name="ssd",
    )(xbc, xbc, xbc, zs, dt_t, conv_w, conv_w, conv_w, conv_b2, conv_b2, conv_b2,
      per_head(dt_bias), per_head(a_log), dskip_l, ng)


def _swa_kernel(q_ref, k_ref, v_ref, pos_ref, invf_ref, sink_ref, o_ref, kprev_ref, vprev_ref):
    w = ATTN_WINDOW
    dh = ATTN_HEAD_DIM
    half = dh // 2
    lanes = 2 * dh
    n_kv = k_ref.shape[1] // dh
    tiles_per_kv = ATTN_GROUP // 2
    nb = pl.program_id(0)

    @pl.when(nb == 0)
    def _():
        kprev_ref[...] = jnp.zeros_like(kprev_ref)
        vprev_ref[...] = jnp.zeros_like(vprev_ref)

    lane = lax.broadcasted_iota(jnp.int32, (w, lanes), 1)
    first_half = (lane % dh) < half
    lo2 = lax.broadcasted_iota(jnp.int32, (2 * w, lanes), 1) < dh

    ang = pos_ref[...].astype(F32) * invf_ref[...]
    cos = jnp.cos(ang)
    sin = jnp.sin(ang)
    sin = jnp.where(first_half, -sin, sin)
    scale = dh ** -0.5

    def rope(u, c, s):
        partner = jnp.where(first_half, pltpu.roll(u, lanes - half, 1), pltpu.roll(u, half, 1))
        return u * c + partner * s

    kj = lax.broadcasted_iota(jnp.int32, (2 * w, w), 0)
    qi = lax.broadcasted_iota(jnp.int32, (2 * w, w), 1)
    valid = (kj > qi) & (kj <= qi + w) & ((kj >= w) | (nb > 0))
    bias = jnp.where(valid, 0.0, -jnp.inf)

    cos_q, sin_q = cos * (scale * LOG2_E), sin * (scale * LOG2_E)
    q_tiles = [rope(q_ref[:, t * lanes:(t + 1) * lanes], cos_q, sin_q).astype(BF16)
               for t in range(q_ref.shape[1] // lanes)]

    units = []
    for pt in range(n_kv // 2):
        sl = slice(pt * lanes, (pt + 1) * lanes)
        k_cur = rope(k_ref[:, sl], cos, sin)
        v_cur = v_ref[:, sl]
        k_pair = jnp.concatenate([kprev_ref[:, sl], k_cur], axis=0)
        v_pair = jnp.concatenate([vprev_ref[:, sl], v_cur], axis=0)
        kprev_ref[:, sl] = k_cur
        vprev_ref[:, sl] = v_cur
        k_swap = pltpu.roll(k_pair, dh, 1)
        vt_pair = v_pair.T
        for side in range(2):
            hk = 2 * pt + side
            k_lo = jnp.where(lo2, k_pair if side == 0 else k_swap, 0.0).astype(BF16)
            k_hi = jnp.where(lo2, 0.0, k_swap if side == 0 else k_pair).astype(BF16)
            vt = vt_pair[side * dh:(side + 1) * dh, :].astype(BF16)
            units += [(hk, 0, k_lo, vt), (hk, 1, k_hi, vt)]

    scores = []
    for hk, parity, k_sel, vt in units:
        qs = jnp.concatenate(q_tiles[hk * tiles_per_kv:(hk + 1) * tiles_per_kv], axis=0)
        scores.append(_dot_nt(k_sel, qs))

    probs = []
    for (hk, parity, k_sel, vt), st in zip(units, scores):
        p_parts, den_parts = [], []
        for j in range(tiles_per_kv):
            head = hk * ATTN_GROUP + 2 * j + parity
            s = st[:, j * w:(j + 1) * w] + bias
            sink = sink_ref[0:1, head:head + 1] * LOG2_E
            m = jnp.maximum(jnp.max(s, axis=0, keepdims=True), sink)
            p = jnp.exp2(s - m)
            den_parts.append(jnp.sum(p, axis=0, keepdims=True) + jnp.exp2(sink - m))
            p_parts.append(p.astype(BF16))
        probs.append((jnp.concatenate(p_parts, axis=1), jnp.concatenate(den_parts, axis=1)))

    outs = {}
    for (hk, parity, k_sel, vt), (pt_all, den) in zip(units, probs):
        outs[hk, parity] = _dot(vt, pt_all) / den

    for hk in range(n_kv):
        for j in range(tiles_per_kv):
            qt = hk * tiles_per_kv + j
            tile_t = jnp.concatenate([outs[hk, 0][:, j * w:(j + 1) * w],
                                      outs[hk, 1][:, j * w:(j + 1) * w]], axis=0)
            o_ref[:, qt * lanes:(qt + 1) * lanes] = tile_t.T.astype(o_ref.dtype)


def _swa(proj, positions, sinks, *, q_col, q_width, k_col, v_col, kv_width):
    t = proj.shape[0]
    w = ATTN_WINDOW
    assert t % w == 0 and q_col % q_width == 0 and k_col % kv_width == 0 and v_col % kv_width == 0
    half = ATTN_HEAD_DIM // 2
    inv_freq = ROPE_THETA ** (-jnp.arange(half, dtype=F32) * 2.0 / ATTN_HEAD_DIM)
    inv_freq = jnp.tile(inv_freq, 2 * V7X_LANES // ATTN_HEAD_DIM).reshape(1, V7X_LANES)
    block_bytes = 2 * (w * q_width * 4 + 2 * w * kv_width * 4 + w * q_width * 2) + 2 * w * kv_width * 4
    return pl.pallas_call(
        _swa_kernel,
        grid=(t // w,),
        in_specs=[
            pl.BlockSpec((w, q_width), lambda i: (i, q_col // q_width)),
            pl.BlockSpec((w, kv_width), lambda i: (i, k_col // kv_width)),
            pl.BlockSpec((w, kv_width), lambda i: (i, v_col // kv_width)),
            pl.BlockSpec((w, 1), lambda i: (i, 0)),
            pl.BlockSpec((1, V7X_LANES), lambda i: (0, 0)),
            pl.BlockSpec((1, sinks.shape[0]), lambda i: (0, 0)),
        ],
        out_specs=pl.BlockSpec((w, q_width), lambda i: (i, 0)),
        out_shape=jax.ShapeDtypeStruct((t, q_width), BF16),
        scratch_shapes=[pltpu.VMEM((w, kv_width), F32), pltpu.VMEM((w, kv_width), F32)],
        compiler_params=pltpu.CompilerParams(
            dimension_semantics=("arbitrary",), vmem_limit_bytes=_vmem_limit(block_bytes)),
        name="swa",
    )(proj, proj, proj, positions.reshape(t, 1), inv_freq, sinks.reshape(1, -1))


def _merge_kernel(ys_ref, ya_ref, ws_ref, wa_ref, gs_ref, ga_ref, o_ref, wsb_ref, wab_ref):
    @pl.when(pl.program_id(1) == 0)
    def _():
        _cast_weight(wsb_ref, ws_ref)
        _cast_weight(wab_ref, wa_ref)

    y_s = _dot(ys_ref[...], wsb_ref[...])
    y_a = _dot(ya_ref[...], wab_ref[...])
    merged = jax.nn.sigmoid(gs_ref[...]) * y_s + jax.nn.sigmoid(ga_ref[...]) * y_a
    o_ref[...] = merged.astype(o_ref.dtype)


def _merge(y_ssd, y_attn, w_ssd_o, w_attn_o, proj, *, gs_col, ga_col, tm, tn):
    t, ks = y_ssd.shape
    ka = y_attn.shape[1]
    d = w_ssd_o.shape[1]
    assert t % tm == 0 and d % tn == 0 and gs_col % tn == 0 and ga_col % tn == 0
    block_bytes = (2 * (tm * ks * 2 + tm * ka * 2 + ks * tn * 4 + ka * tn * 4 + 2 * tm * tn * 4 + tm * tn * 2)
                   + (ks + ka) * tn * 2)
    return pl.pallas_call(
        _merge_kernel,
        grid=(d // tn, t // tm),
        in_specs=[pl.BlockSpec((tm, ks), lambda j, i: (i, 0)),
                  pl.BlockSpec((tm, ka), lambda j, i: (i, 0)),
                  pl.BlockSpec((ks, tn), lambda j, i: (0, j)),
                  pl.BlockSpec((ka, tn), lambda j, i: (0, j)),
                  pl.BlockSpec((tm, tn), lambda j, i: (i, gs_col // tn + j)),
                  pl.BlockSpec((tm, tn), lambda j, i: (i, ga_col // tn + j))],
        out_specs=pl.BlockSpec((tm, tn), lambda j, i: (i, j)),
        out_shape=jax.ShapeDtypeStruct((t, d), BF16),
        scratch_shapes=[pltpu.VMEM((ks, tn), BF16), pltpu.VMEM((ka, tn), BF16)],
        compiler_params=pltpu.CompilerParams(
            dimension_semantics=("parallel", "arbitrary"), vmem_limit_bytes=_vmem_limit(block_bytes)),
        name="merge",
    )(y_ssd, y_attn, w_ssd_o, w_attn_o, proj, proj)


def _proj_ln_kernel(m_ref, w_ref, h_ref, g_ref, b_ref, o_ref, wb_ref):
    @pl.when(pl.program_id(0) == 0)
    def _():
        _cast_weight(wb_ref, w_ref)

    y = DEEPNORM_ALPHA * h_ref[...] + _dot(m_ref[...], wb_ref[...])
    o_ref[...] = _layer_norm(y, g_ref[...], b_ref[...])


def _proj_ln(merged, w_out, h, g, b, *, tm):
    t, d = h.shape
    k = merged.shape[1]
    assert t % tm == 0
    vec = pl.BlockSpec((1, d), lambda i: (0, 0))
    block_bytes = 2 * (tm * k * 2 + 2 * tm * d * 4) + k * d * 4 + k * d * 2
    return pl.pallas_call(
        _proj_ln_kernel,
        grid=(t // tm,),
        in_specs=[pl.BlockSpec((tm, k), lambda i: (i, 0)),
                  pl.BlockSpec((k, d), lambda i: (0, 0), pipeline_mode=pl.Buffered(1)),
                  pl.BlockSpec((tm, d), lambda i: (i, 0)),
                  vec, vec],
        out_specs=pl.BlockSpec((tm, d), lambda i: (i, 0)),
        out_shape=jax.ShapeDtypeStruct((t, d), F32),
        scratch_shapes=[pltpu.VMEM((k, d), BF16)],
        compiler_params=pltpu.CompilerParams(
            dimension_semantics=("arbitrary",), vmem_limit_bytes=_vmem_limit(block_bytes)),
        name="proj_ln",
    )(merged, w_out, h, g, b)


def _layer(x, positions, ffn1_w_gate, ffn1_w_up, ffn1_w_down, ln1_g, ln1_b,
           w_in, conv_w, conv_b, dt_bias, a_log, d_skip, ssd_norm_g, w_ssd_o,
           attn_sinks, w_attn_o, w_out, ln2_g, ln2_b,
           ffn2_w_gate, ffn2_w_up, ffn2_w_down, ln3_g, ln3_b):
    t, d = x.shape
    d_inner = w_ssd_o.shape[0]
    n_ssd_heads = dt_bias.shape[0]
    q_width = w_attn_o.shape[0]
    kv_width = q_width // ATTN_GROUP
    xbc_width = d_inner + 2 * SSD_N_GROUPS * SSD_D_STATE
    vec = lambda v: v.reshape(1, -1)
    tm_ffn = min(512, t)
    tm_mm = min(1024, t)

    sizes = (d_inner, xbc_width, n_ssd_heads, q_width, kv_width, kv_width, d, d)
    starts = [0]
    for s in sizes:
        starts.append(starts[-1] + s)
    dt_lo, dt_hi, total = starts[2], starts[3], starts[-1]

    h1, h1b, wg2, wu2, wd2 = _ffn_ln(
        x, ffn1_w_gate.astype(BF16), ffn1_w_up.astype(BF16), ffn1_w_down.astype(BF16), vec(ln1_g), vec(ln1_b),
        tm=tm_ffn, th=512, emit_bf16=True, name="ffn1_ln", side_cast=(ffn2_w_gate, ffn2_w_up, ffn2_w_down))

    w_in_t = w_in.T
    n_qkv = q_width + 2 * kv_width
    zs = _in_proj(h1b, w_in_t, row0=0, n=d_inner, tm=tm_mm, tn=1024, weight_buffers=2, name="in_proj_z",
                  apply_silu=True)
    xbc = _in_proj(h1b, w_in_t, row0=d_inner, n=xbc_width, tm=tm_mm, tn=1024, weight_buffers=2,
                   name="in_proj_xbc")
    qkv = _in_proj(h1b, w_in_t, row0=dt_hi, n=n_qkv, tm=tm_mm, tn=n_qkv // 2, weight_buffers=2,
                   name="in_proj_qkv")
    gates = _in_proj(h1b, w_in_t, row0=dt_hi + n_qkv, n=2 * d, tm=tm_mm, tn=1024, weight_buffers=2,
                     name="in_proj_gates")
    dt_t = _dt_proj(h1b, w_in_t, row0=dt_lo, n_heads=n_ssd_heads, tm=tm_mm)

    y_ssd = _ssd(zs, xbc, dt_t, conv_w, conv_b, dt_bias, a_log, d_skip, ssd_norm_g,
                 d_inner=d_inner, rows=min(SSD_ROWS_PER_STEP, t))
    y_attn = _swa(qkv, positions, attn_sinks, q_col=0, q_width=q_width,
                  k_col=q_width, v_col=q_width + kv_width, kv_width=kv_width)

    merged = _merge(y_ssd, y_attn, w_ssd_o, w_attn_o, gates, gs_col=0, ga_col=d, tm=tm_ffn, tn=512)
    h2 = _proj_ln(merged, w_out, h1, vec(ln2_g), vec(ln2_b), tm=tm_ffn)

    (out,) = _ffn_ln(h2, wg2, wu2, wd2, vec(ln3_g), vec(ln3_b), tm=tm_ffn, th=512, emit_bf16=False, name="ffn2_ln")
    return out


def kernel(x, positions, ffn1_w_gate, ffn1_w_up, ffn1_w_down, ln1_g, ln1_b, w_in, conv_w, conv_b, dt_bias, a_log, d_skip, ssd_norm_g, w_ssd_o, attn_sinks, w_attn_o, w_out, ln2_g, ln2_b, ffn2_w_gate, ffn2_w_up, ffn2_w_down, ln3_g, ln3_b):
    batch, depth = x.shape[0], ffn1_w_gate.shape[0]
    assert depth == DEPTH
    outs = []
    for bi in range(batch):
        h = x[bi]
        for l in range(depth):
            h = _layer(h, positions[bi], ffn1_w_gate[l], ffn1_w_up[l], ffn1_w_down[l], ln1_g[l], ln1_b[l],
                       w_in[l], conv_w[l], conv_b[l], dt_bias[l], a_log[l], d_skip[l], ssd_norm_g[l], w_ssd_o[l],
                       attn_sinks[l], w_attn_o[l], w_out[l], ln2_g[l], ln2_b[l],
                       ffn2_w_gate[l], ffn2_w_up[l], ffn2_w_down[l], ln3_g[l], ln3_b[l])
        outs.append(h)
    return jnp.stack(outs, axis=0)
```

```python
import functools
import math

import jax
import jax.numpy as jnp
from jax import lax
from jax.experimental import pallas as pl
from jax.experimental.pallas import tpu as pltpu

F32 = jnp.float32
BF16 = jnp.bfloat16

V7X_LANES = 128
V7X_SUBLANES = 8
V7X_VMEM_BYTES = 64 * 1024 * 1024

SSD_HEAD_DIM = 64
SSD_N_GROUPS = 8
SSD_D_STATE = 128
SSD_CONV_WIDTH = 4
SSD_CHUNK = 128
ATTN_HEAD_DIM = 64
ATTN_GROUP = 8
ATTN_WINDOW = 128
ROPE_THETA = 10000.0
DEPTH = 1
DEEPNORM_ALPHA = (2 * DEPTH) ** 0.25
LN_EPS = 1e-5
RMS_EPS = 1e-5
LOG2_E = math.log2(math.e)

CAST_ROWS = 256
SSD_ROWS_PER_STEP = 8 * SSD_CHUNK
FFN_HIDDEN_SLICES = 2


def _vmem_limit(block_bytes):
    return int(min(block_bytes * 5 // 4 + (8 << 20), V7X_VMEM_BYTES - (6 << 20)))


def _dot(a, b):
    return jnp.dot(a, b, preferred_element_type=F32)


def _dot_nt(a, b):
    return lax.dot_general(a, b, (((1,), (1,)), ((), ())), preferred_element_type=F32)


def _dot_tn(a, b):
    return lax.dot_general(a, b, (((0,), (0,)), ((), ())), preferred_element_type=F32)


def _dot_f32(a, b):
    return jnp.dot(a, b, preferred_element_type=F32, precision=lax.Precision.HIGHEST)


def _layer_norm(y, g, b):
    mu = jnp.mean(y, axis=-1, keepdims=True)
    yc = y - mu
    var = jnp.mean(yc * yc, axis=-1, keepdims=True)
    return yc * lax.rsqrt(var + LN_EPS) * g + b


def _silu(x):
    return x * jax.nn.sigmoid(x)


def _cast_weight(dst_ref, w_ref):
    k = dst_ref.shape[0]
    for r0 in range(0, k, CAST_ROWS):
        rows = slice(r0, min(r0 + CAST_ROWS, k))
        dst_ref[rows, :] = w_ref[rows, :].astype(BF16)


def _ffn_ln_kernel(x_ref, wg_ref, wu_ref, wd_ref, g_ref, b_ref, *refs, n_hidden_tiles, emit_bf16, n_side):
    side_in, refs = refs[:n_side], refs[n_side:]
    if emit_bf16:
        o_ref, ob_ref = refs[:2]
        refs = refs[2:]
    else:
        o_ref = refs[0]
        refs = refs[1:]
    side_out, (xb_ref, acc_ref) = refs[:n_side], refs[n_side:]
    j = pl.program_id(1)

    for src_ref, dst_ref in zip(side_in, side_out):
        dst_ref[...] = src_ref[...].astype(BF16)

    @pl.when(j == 0)
    def _():
        xb_ref[...] = x_ref[...].astype(BF16)
        acc_ref[...] = jnp.zeros_like(acc_ref)

    xb = xb_ref[...]
    th = wg_ref.shape[1]
    partial = None
    for h0 in range(0, th, th // FFN_HIDDEN_SLICES):
        cols = slice(h0, h0 + th // FFN_HIDDEN_SLICES)
        gate = _dot(xb, wg_ref[:, cols])
        up = _dot(xb, wu_ref[:, cols])
        act = (_silu(gate) * up).astype(BF16)
        down = _dot(act, wd_ref[cols, :])
        partial = down if partial is None else partial + down
    acc_ref[...] += partial

    @pl.when(j == n_hidden_tiles - 1)
    def _():
        y = DEEPNORM_ALPHA * x_ref[...] + 0.5 * acc_ref[...]
        out = _layer_norm(y, g_ref[...], b_ref[...])
        o_ref[...] = out
        if emit_bf16:
            ob_ref[...] = out.astype(BF16)


def _ffn_ln(x, wg, wu, wd, g, b, *, tm, th, emit_bf16, name, side_cast=()):
    t, d = x.shape
    hidden = wg.shape[1]
    assert t % tm == 0 and hidden % th == 0
    n_i, n_j = t // tm, hidden // th
    row = pl.BlockSpec((tm, d), lambda i, j: (i, 0))
    vec = pl.BlockSpec((1, d), lambda i, j: (0, 0))
    out_shape = [jax.ShapeDtypeStruct((t, d), F32)]
    out_specs = [row]
    if emit_bf16:
        out_shape.append(jax.ShapeDtypeStruct((t, d), BF16))
        out_specs.append(row)
    block_bytes = (2 * tm * d * 4 + 2 * 3 * d * th * 2 + 2 * tm * d * 4 + (2 * tm * d * 2 if emit_bf16 else 0)
                   + tm * d * 2 + tm * d * 4)
    side_specs = []
    for w in side_cast:
        r, c = w.shape
        if r % n_i == 0 and c % n_j == 0 and (c // n_j) % V7X_LANES == 0:
            spec = pl.BlockSpec((r // n_i, c // n_j), lambda i, j: (i, j))
        else:
            assert r % n_j == 0 and c % n_i == 0 and (c // n_i) % V7X_LANES == 0
            spec = pl.BlockSpec((r // n_j, c // n_i), lambda i, j: (j, i))
        side_specs.append(spec)
        out_shape.append(jax.ShapeDtypeStruct((r, c), BF16))
        block_bytes += 2 * (r * c // (n_i * n_j)) * (4 + 2)
    return pl.pallas_call(
        functools.partial(_ffn_ln_kernel, n_hidden_tiles=n_j, emit_bf16=emit_bf16, n_side=len(side_cast)),
        grid=(n_i, n_j),
        in_specs=[row,
                  pl.BlockSpec((d, th), lambda i, j: (0, j)),
                  pl.BlockSpec((d, th), lambda i, j: (0, j)),
                  pl.BlockSpec((th, d), lambda i, j: (j, 0)),
                  vec, vec] + side_specs,
        out_specs=out_specs + side_specs,
        out_shape=out_shape,
        scratch_shapes=[pltpu.VMEM((tm, d), BF16), pltpu.VMEM((tm, d), F32)],
        compiler_params=pltpu.CompilerParams(
            dimension_semantics=("parallel", "arbitrary"), vmem_limit_bytes=_vmem_limit(block_bytes)),
        name=name,
    )(x, wg, wu, wd, g, b, *side_cast)


def _in_proj_kernel(a_ref, w_ref, o_ref, wb_ref, *, apply_silu):
    @pl.when(pl.program_id(1) == 0)
    def _():
        _cast_weight(wb_ref, w_ref)

    raw = _dot_nt(a_ref[...], wb_ref[...])
    o_ref[...] = _silu(raw) if apply_silu else raw


def _in_proj(a, w_t, *, row0, n, tm, tn, weight_buffers, name, apply_silu=False):
    m, k = a.shape
    assert m % tm == 0 and n % tn == 0 and row0 % V7X_SUBLANES == 0
    w_spec = pl.BlockSpec((pl.Element(tn), pl.Element(k)),
                          lambda j, i: (pl.multiple_of(row0 + j * tn, V7X_SUBLANES), 0),
                          pipeline_mode=pl.Buffered(weight_buffers))
    block_bytes = 2 * (tm * k * 2 + tm * tn * 4) + weight_buffers * tn * k * 4 + tn * k * 2
    return pl.pallas_call(
        functools.partial(_in_proj_kernel, apply_silu=apply_silu),
        grid=(n // tn, m // tm),
        in_specs=[pl.BlockSpec((tm, k), lambda j, i: (i, 0)), w_spec],
        out_specs=pl.BlockSpec((tm, tn), lambda j, i: (i, j)),
        out_shape=jax.ShapeDtypeStruct((m, n), F32),
        scratch_shapes=[pltpu.VMEM((tn, k), BF16)],
        compiler_params=pltpu.CompilerParams(
            dimension_semantics=("parallel", "arbitrary"), vmem_limit_bytes=_vmem_limit(block_bytes)),
        name=name,
    )(a, w_t)


def _dt_proj_kernel(a_ref, w_ref, o_ref):
    o_ref[...] = _dot_nt(w_ref[...].astype(BF16), a_ref[...])


def _dt_proj(a, w_t, *, row0, n_heads, tm):
    m, k = a.shape
    assert row0 % n_heads == 0 and n_heads % V7X_SUBLANES == 0 and m % tm == 0
    block_bytes = 2 * (tm * k * 2 + n_heads * k * 4 + n_heads * tm * 4)
    return pl.pallas_call(
        _dt_proj_kernel,
        grid=(m // tm,),
        in_specs=[pl.BlockSpec((tm, k), lambda i: (i, 0)),
                  pl.BlockSpec((n_heads, k), lambda i: (row0 // n_heads, 0))],
        out_specs=pl.BlockSpec((n_heads, tm), lambda i: (0, i)),
        out_shape=jax.ShapeDtypeStruct((n_heads, m), F32),
        compiler_params=pltpu.CompilerParams(
            dimension_semantics=("parallel",), vmem_limit_bytes=_vmem_limit(block_bytes)),
        name="dt_proj",
    )(a, w_t)


def _ssd_kernel(x_ref, b_ref, c_ref, z_ref, dt_ref,
                cwx_ref, cwb_ref, cwc_ref, cbx_ref, cbb_ref, cbc_ref,
                bias_ref, alog_ref, dskip_ref, ng_ref,
                y_ref, xpad_ref, bpad_ref, cpad_ref, xs_ref, bm_ref, cm_ref, state_ref):
    q = SSD_CHUNK
    tail = V7X_SUBLANES
    rows = x_ref.shape[0]
    gw = x_ref.shape[1]
    heads = gw // SSD_HEAD_DIM
    pairs = heads // 2
    pw = 2 * SSD_HEAD_DIM

    @pl.when(pl.program_id(1) == 0)
    def _():
        xpad_ref[0:tail, :] = jnp.zeros((tail, xpad_ref.shape[1]), F32)
        bpad_ref[0:tail, :] = jnp.zeros((tail, bpad_ref.shape[1]), F32)
        cpad_ref[0:tail, :] = jnp.zeros((tail, cpad_ref.shape[1]), F32)
        state_ref[...] = jnp.zeros_like(state_ref)

    def conv_silu(u_ref, pad_ref, w_ref, cbias_ref, dst_ref):
        pad_ref[tail:tail + rows, :] = u_ref[...]
        acc = cbias_ref[...]
        for k in range(SSD_CONV_WIDTH):
            start = tail - (SSD_CONV_WIDTH - 1) + k
            acc = acc + w_ref[k:k + 1, :] * pad_ref[start:start + rows, :]
        pad_ref[0:tail, :] = pad_ref[rows:rows + tail, :]
        dst_ref[...] = _silu(acc).astype(dst_ref.dtype)

    conv_silu(x_ref, xpad_ref, cwx_ref, cbx_ref, xs_ref)
    conv_silu(b_ref, bpad_ref, cwb_ref, cbb_ref, bm_ref)
    conv_silu(c_ref, cpad_ref, cwc_ref, cbc_ref, cm_ref)

    dt_all = jax.nn.softplus(dt_ref[...] + bias_ref[...])
    adt_all = dt_all * (-jnp.exp(alog_ref[...]) * LOG2_E)

    ri = lax.broadcasted_iota(jnp.int32, (q, q), 0)
    ci = lax.broadcasted_iota(jnp.int32, (q, q), 1)
    causal = ci <= ri
    upper = (ri <= ci).astype(F32)
    lo = lax.broadcasted_iota(jnp.int32, (q, pw), 1) < SSD_HEAD_DIM
    lo_row = lax.broadcasted_iota(jnp.int32, (1, pw), 1) < SSD_HEAD_DIM

    staged = []
    for ck in range(rows // q):
        r = slice(ck * q, (ck + 1) * q)
        dt_r = dt_all[:, r]
        cs_r = _dot_f32(adt_all[:, r], upper)

        def over_lanes(row0, row1):
            stacked = jnp.concatenate([jnp.broadcast_to(row0, (SSD_HEAD_DIM, q)),
                                       jnp.broadcast_to(row1, (SSD_HEAD_DIM, q))], axis=0)
            return stacked.T

        cs_l = [over_lanes(cs_r[h:h + 1, :], cs_r[h:h + 1, :]) for h in range(heads)]

        bm_b = bm_ref[r, :]
        cm_b = cm_ref[r, :]
        cb = _dot_nt(cm_b, bm_b)

        y_parts, din_parts, xdd_parts, cd_parts = [], [], [], []
        for p in range(pairs):
            h0, h1 = 2 * p, 2 * p + 1
            sl = slice(p * pw, (p + 1) * pw)
            xp = xs_ref[r, sl]
            xd = xp * over_lanes(dt_r[h0:h0 + 1, :], dt_r[h1:h1 + 1, :])
            xd_b = xd.astype(BF16)
            y_diag = []
            for h in (h0, h1):
                seg = jnp.where(causal, jnp.exp2(cs_l[h] - cs_r[h:h + 1, :]), 0.0)
                y_diag.append(_dot((cb * seg).astype(BF16), xd_b))
            cs_p = jnp.where(lo, cs_l[h0], cs_l[h1])
            cs_end = jnp.where(lo_row, cs_l[h0][q - 1:q, :], cs_l[h1][q - 1:q, :])
            y_parts.append(jnp.where(lo, y_diag[0], y_diag[1]) + dskip_ref[:, sl] * xp)
            din_parts.append(jnp.exp2(cs_p))
            xdd_parts.append((xd * jnp.exp2(cs_end - cs_p)).astype(BF16))
            cd_parts.append(jnp.exp2(cs_end))
        staged.append((cm_b, bm_b, jnp.concatenate(y_parts, axis=1), jnp.concatenate(din_parts, axis=1),
                       jnp.concatenate(xdd_parts, axis=1), jnp.concatenate(cd_parts, axis=1)))

    for ck, (cm_b, bm_b, y_local, decay_in, xdd, chunk_decay) in enumerate(staged):
        r = slice(ck * q, (ck + 1) * q)
        state = state_ref[...]
        y = y_local + _dot(cm_b, state.astype(BF16)) * decay_in
        state_ref[...] = state * chunk_decay + _dot_tn(bm_b, xdd)
        y = y * z_ref[r, :]
        y = y * lax.rsqrt(jnp.mean(y * y, axis=-1, keepdims=True) + RMS_EPS) * ng_ref[...]
        y_ref[r, :] = y.astype(y_ref.dtype)


def _ssd(zs, xbc, dt_t, conv_w, conv_b, dt_bias, a_log, d_skip, norm_g, *, d_inner, rows):
    t = zs.shape[0]
    g = SSD_N_GROUPS
    gw = d_inner // g
    hg = gw // SSD_HEAD_DIM
    n = SSD_D_STATE
    assert t % rows == 0 and rows % SSD_CHUNK == 0 and d_inner % n == 0
    assert 2 * hg <= SSD_CHUNK and hg % V7X_SUBLANES == 0
    conv_b2 = conv_b.reshape(1, -1)
    per_head = lambda v: v.reshape(g * hg, 1)
    dskip_l = jnp.repeat(d_skip, SSD_HEAD_DIM).reshape(1, d_inner)
    ng = norm_g.reshape(1, d_inner)
    b_blk = d_inner // n
    c_blk = b_blk + g

    head_col = pl.BlockSpec((hg, 1), lambda gi, ci: (gi, 0))
    in_specs = [
        pl.BlockSpec((rows, gw), lambda gi, ci: (ci, gi)),
        pl.BlockSpec((rows, n), lambda gi, ci: (ci, b_blk + gi)),
        pl.BlockSpec((rows, n), lambda gi, ci: (ci, c_blk + gi)),
        pl.BlockSpec((rows, gw), lambda gi, ci: (ci, gi)),
        pl.BlockSpec((hg, rows), lambda gi, ci: (gi, ci)),
        pl.BlockSpec((SSD_CONV_WIDTH, gw), lambda gi, ci: (0, gi)),
        pl.BlockSpec((SSD_CONV_WIDTH, n), lambda gi, ci: (0, b_blk + gi)),
        pl.BlockSpec((SSD_CONV_WIDTH, n), lambda gi, ci: (0, c_blk + gi)),
        pl.BlockSpec((1, gw), lambda gi, ci: (0, gi)),
        pl.BlockSpec((1, n), lambda gi, ci: (0, b_blk + gi)),
        pl.BlockSpec((1, n), lambda gi, ci: (0, c_blk + gi)),
        head_col, head_col,
        pl.BlockSpec((1, gw), lambda gi, ci: (0, gi)),
        pl.BlockSpec((1, gw), lambda gi, ci: (0, gi)),
    ]
    pad_rows = rows + V7X_SUBLANES
    block_bytes = (2 * (2 * rows * gw * 4 + 2 * rows * n * 4 + rows * gw * 2) + pad_rows * (gw + 2 * n) * 4
                   + rows * gw * 4 + 2 * rows * n * 2 + n * gw * 4)
    return pl.pallas_call(
        _ssd_kernel,
        grid=(g, t // rows),
        in_specs=in_specs,
        out_specs=pl.BlockSpec((rows, gw), lambda gi, ci: (ci, gi)),
        out_shape=jax.ShapeDtypeStruct((t, d_inner), BF16),
        scratch_shapes=[pltpu.VMEM((pad_rows, gw), F32), pltpu.VMEM((pad_rows, n), F32),
                        pltpu.VMEM((pad_rows, n), F32), pltpu.VMEM((rows, gw), F32),
                        pltpu.VMEM((rows, n), BF16), pltpu.VMEM((rows, n), BF16),
                        pltpu.VMEM((n, gw), F32)],
        compiler_params=pltpu.CompilerParams(
            dimension_semantics=("parallel", "arbitrary"), vmem_limit_bytes=_vmem_limit(block_bytes)),
        name="ssd",
    )(xbc, xbc, xbc, zs, dt_t, conv_w, conv_w, conv_w, conv_b2, conv_b2, conv_b2,
      per_head(dt_bias), per_head(a_log), dskip_l, ng)


def _swa_kernel(q_ref, k_ref, v_ref, pos_ref, invf_ref, sink_ref, o_ref, kprev_ref, vprev_ref):
    w = ATTN_WINDOW
    dh = ATTN_HEAD_DIM
    half = dh // 2
    lanes = 2 * dh
    n_kv = k_ref.shape[1] // dh
    tiles_per_kv = ATTN_GROUP // 2
    nb = pl.program_id(0)

    @pl.when(nb == 0)
    def _():
        kprev_ref[...] = jnp.zeros_like(kprev_ref)
        vprev_ref[...] = jnp.zeros_like(vprev_ref)

    lane = lax.broadcasted_iota(jnp.int32, (w, lanes), 1)
    first_half = (lane % dh) < half
    lo2 = lax.broadcasted_iota(jnp.int32, (2 * w, lanes), 1) < dh

    ang = pos_ref[...].astype(F32) * invf_ref[...]
    cos = jnp.cos(ang)
    sin = jnp.sin(ang)
    sin = jnp.where(first_half, -sin, sin)
    scale = dh ** -0.5

    def rope(u, c, s):
        partner = jnp.where(first_half, pltpu.roll(u, lanes - half, 1), pltpu.roll(u, half, 1))
        return u * c + partner * s

    kj = lax.broadcasted_iota(jnp.int32, (2 * w, w), 0)
    qi = lax.broadcasted_iota(jnp.int32, (2 * w, w), 1)
    valid = (kj > qi) & (kj <= qi + w) & ((kj >= w) | (nb > 0))
    bias = jnp.where(valid, 0.0, -jnp.inf)

    cos_q, sin_q = cos * (scale * LOG2_E), sin * (scale * LOG2_E)
    q_tiles = [rope(q_ref[:, t * lanes:(t + 1) * lanes], cos_q, sin_q).astype(BF16)
               for t in range(q_ref.shape[1] // lanes)]

    units = []
    for pt in range(n_kv // 2):
        sl = slice(pt * lanes, (pt + 1) * lanes)
        k_cur = rope(k_ref[:, sl], cos, sin)
        v_cur = v_ref[:, sl]
        k_pair = jnp.concatenate([kprev_ref[:, sl], k_cur], axis=0)
        v_pair = jnp.concatenate([vprev_ref[:, sl], v_cur], axis=0)
        kprev_ref[:, sl] = k_cur
        vprev_ref[:, sl] = v_cur
        k_swap = pltpu.roll(k_pair, dh, 1)
        vt_pair = v_pair.T
        for side in range(2):
            hk = 2 * pt + side
            k_lo = jnp.where(lo2, k_pair if side == 0 else k_swap, 0.0).astype(BF16)
            k_hi = jnp.where(lo2, 0.0, k_swap if side == 0 else k_pair).astype(BF16)
            vt = vt_pair[side * dh:(side + 1) * dh, :].astype(BF16)
            units += [(hk, 0, k_lo, vt), (hk, 1, k_hi, vt)]

    scores = []
    for hk, parity, k_sel, vt in units:
        qs = jnp.concatenate(q_tiles[hk * tiles_per_kv:(hk + 1) * tiles_per_kv], axis=0)
        scores.append(_dot_nt(k_sel, qs))

    probs = []
    for (hk, parity, k_sel, vt), st in zip(units, scores):
        p_parts, den_parts = [], []
        for j in range(tiles_per_kv):
            head = hk * ATTN_GROUP + 2 * j + parity
            s = st[:, j * w:(j + 1) * w] + bias
            sink = sink_ref[0:1, head:head + 1] * LOG2_E
            m = jnp.maximum(jnp.max(s, axis=0, keepdims=True), sink)
            p = jnp.exp2(s - m)
            den_parts.append(jnp.sum(p, axis=0, keepdims=True) + jnp.exp2(sink - m))
            p_parts.append(p.astype(BF16))
        probs.append((jnp.concatenate(p_parts, axis=1), jnp.concatenate(den_parts, axis=1)))

    outs = {}
    for (hk, parity, k_sel, vt), (pt_all, den) in zip(units, probs):
        outs[hk, parity] = _dot(vt, pt_all) / den

    for hk in range(n_kv):
        for j in range(tiles_per_kv):
            qt = hk * tiles_per_kv + j
            tile_t = jnp.concatenate([outs[hk, 0][:, j * w:(j + 1) * w],
                                      outs[hk, 1][:, j * w:(j + 1) * w]], axis=0)
            o_ref[:, qt * lanes:(qt + 1) * lanes] = tile_t.T.astype(o_ref.dtype)


def _swa(proj, positions, sinks, *, q_col, q_width, k_col, v_col, kv_width):
    t = proj.shape[0]
    w = ATTN_WINDOW
    assert t % w == 0 and q_col % q_width == 0 and k_col % kv_width == 0 and v_col % kv_width == 0
    half = ATTN_HEAD_DIM // 2
    inv_freq = ROPE_THETA ** (-jnp.arange(half, dtype=F32) * 2.0 / ATTN_HEAD_DIM)
    inv_freq = jnp.tile(inv_freq, 2 * V7X_LANES // ATTN_HEAD_DIM).reshape(1, V7X_LANES)
    block_bytes = 2 * (w * q_width * 4 + 2 * w * kv_width * 4 + w * q_width * 2) + 2 * w * kv_width * 4
    return pl.pallas_call(
        _swa_kernel,
        grid=(t // w,),
        in_specs=[
            pl.BlockSpec((w, q_width), lambda i: (i, q_col // q_width)),
            pl.BlockSpec((w, kv_width), lambda i: (i, k_col // kv_width)),
            pl.BlockSpec((w, kv_width), lambda i: (i, v_col // kv_width)),
            pl.BlockSpec((w, 1), lambda i: (i, 0)),
            pl.BlockSpec((1, V7X_LANES), lambda i: (0, 0)),
            pl.BlockSpec((1, sinks.shape[0]), lambda i: (0, 0)),
        ],
        out_specs=pl.BlockSpec((w, q_width), lambda i: (i, 0)),
        out_shape=jax.ShapeDtypeStruct((t, q_width), BF16),
        scratch_shapes=[pltpu.VMEM((w, kv_width), F32), pltpu.VMEM((w, kv_width), F32)],
        compiler_params=pltpu.CompilerParams(
            dimension_semantics=("arbitrary",), vmem_limit_bytes=_vmem_limit(block_bytes)),
        name="swa",
    )(proj, proj, proj, positions.reshape(t, 1), inv_freq, sinks.reshape(1, -1))


def _merge_kernel(ys_ref, ya_ref, ws_ref, wa_ref, gs_ref, ga_ref, o_ref, wsb_ref, wab_ref):
    @pl.when(pl.program_id(1) == 0)
    def _():
        _cast_weight(wsb_ref, ws_ref)
        _cast_weight(wab_ref, wa_ref)

    tn = o_ref.shape[1]
    for c0 in range(0, tn, tn // 2):
        cols = slice(c0, c0 + tn // 2)
        y_s = _dot(ys_ref[...], wsb_ref[:, cols])
        y_a = _dot(ya_ref[...], wab_ref[:, cols])
        merged = jax.nn.sigmoid(gs_ref[:, cols]) * y_s + jax.nn.sigmoid(ga_ref[:, cols]) * y_a
        o_ref[:, cols] = merged.astype(o_ref.dtype)


def _merge(y_ssd, y_attn, w_ssd_o, w_attn_o, proj, *, gs_col, ga_col, tm, tn):
    t, ks = y_ssd.shape
    ka = y_attn.shape[1]
    d = w_ssd_o.shape[1]
    assert t % tm == 0 and d % tn == 0 and gs_col % tn == 0 and ga_col % tn == 0
    block_bytes = (2 * (tm * ks * 2 + tm * ka * 2 + ks * tn * 4 + ka * tn * 4 + 2 * tm * tn * 4 + tm * tn * 2)
                   + (ks + ka) * tn * 2)
    return pl.pallas_call(
        _merge_kernel,
        grid=(d // tn, t // tm),
        in_specs=[pl.BlockSpec((tm, ks), lambda j, i: (i, 0)),
                  pl.BlockSpec((tm, ka), lambda j, i: (i, 0)),
                  pl.BlockSpec((ks, tn), lambda j, i: (0, j)),
                  pl.BlockSpec((ka, tn), lambda j, i: (0, j)),
                  pl.BlockSpec((tm, tn), lambda j, i: (i, gs_col // tn + j)),
                  pl.BlockSpec((tm, tn), lambda j, i: (i, ga_col // tn + j))],
        out_specs=pl.BlockSpec((tm, tn), lambda j, i: (i, j)),
        out_shape=jax.ShapeDtypeStruct((t, d), BF16),
        scratch_shapes=[pltpu.VMEM((ks, tn), BF16), pltpu.VMEM((ka, tn), BF16)],
        compiler_params=pltpu.CompilerParams(
            dimension_semantics=("parallel", "arbitrary"), vmem_limit_bytes=_vmem_limit(block_bytes)),
        name="merge",
    )(y_ssd, y_attn, w_ssd_o, w_attn_o, proj, proj)


def _proj_ln_kernel(m_ref, w_ref, h_ref, g_ref, b_ref, o_ref, wb_ref):
    @pl.when(pl.program_id(0) == 0)
    def _():
        _cast_weight(wb_ref, w_ref)

    y = DEEPNORM_ALPHA * h_ref[...] + _dot(m_ref[...], wb_ref[...])
    o_ref[...] = _layer_norm(y, g_ref[...], b_ref[...])


def _proj_ln(merged, w_out, h, g, b, *, tm):
    t, d = h.shape
    k = merged.shape[1]
    assert t % tm == 0
    vec = pl.BlockSpec((1, d), lambda i: (0, 0))
    block_bytes = 2 * (tm * k * 2 + 2 * tm * d * 4) + k * d * 4 + k * d * 2
    return pl.pallas_call(
        _proj_ln_kernel,
        grid=(t // tm,),
        in_specs=[pl.BlockSpec((tm, k), lambda i: (i, 0)),
                  pl.BlockSpec((k, d), lambda i: (0, 0), pipeline_mode=pl.Buffered(1)),
                  pl.BlockSpec((tm, d), lambda i: (i, 0)),
                  vec, vec],
        out_specs=pl.BlockSpec((tm, d), lambda i: (i, 0)),
        out_shape=jax.ShapeDtypeStruct((t, d), F32),
        scratch_shapes=[pltpu.VMEM((k, d), BF16)],
        compiler_params=pltpu.CompilerParams(
            dimension_semantics=("arbitrary",), vmem_limit_bytes=_vmem_limit(block_bytes)),
        name="proj_ln",
    )(merged, w_out, h, g, b)


def _layer(x, positions, ffn1_w_gate, ffn1_w_up, ffn1_w_down, ln1_g, ln1_b,
           w_in, conv_w, conv_b, dt_bias, a_log, d_skip, ssd_norm_g, w_ssd_o,
           attn_sinks, w_attn_o, w_out, ln2_g, ln2_b,
           ffn2_w_gate, ffn2_w_up, ffn2_w_down, ln3_g, ln3_b):
    t, d = x.shape
    d_inner = w_ssd_o.shape[0]
    n_ssd_heads = dt_bias.shape[0]
    q_width = w_attn_o.shape[0]
    kv_width = q_width // ATTN_GROUP
    xbc_width = d_inner + 2 * SSD_N_GROUPS * SSD_D_STATE
    vec = lambda v: v.reshape(1, -1)
    tm_ffn = min(512, t)
    tm_mm = min(1024, t)

    sizes = (d_inner, xbc_width, n_ssd_heads, q_width, kv_width, kv_width, d, d)
    starts = [0]
    for s in sizes:
        starts.append(starts[-1] + s)
    dt_lo, dt_hi, total = starts[2], starts[3], starts[-1]

    h1, h1b, wg2, wu2, wd2 = _ffn_ln(
        x, ffn1_w_gate.astype(BF16), ffn1_w_up.astype(BF16), ffn1_w_down.astype(BF16), vec(ln1_g), vec(ln1_b),
        tm=tm_ffn, th=512, emit_bf16=True, name="ffn1_ln", side_cast=(ffn2_w_gate, ffn2_w_up, ffn2_w_down))

    w_in_t = w_in.T
    n_qkv = q_width + 2 * kv_width
    zs = _in_proj(h1b, w_in_t, row0=0, n=d_inner, tm=tm_mm, tn=1024, weight_buffers=2, name="in_proj_z",
                  apply_silu=True)
    xbc = _in_proj(h1b, w_in_t, row0=d_inner, n=xbc_width, tm=tm_mm, tn=1024, weight_buffers=2,
                   name="in_proj_xbc")
    qkv = _in_proj(h1b, w_in_t, row0=dt_hi, n=n_qkv, tm=tm_mm, tn=n_qkv // 2, weight_buffers=2,
                   name="in_proj_qkv")
    gates = _in_proj(h1b, w_in_t, row0=dt_hi + n_qkv, n=2 * d, tm=tm_mm, tn=1024, weight_buffers=2,
                     name="in_proj_gates")
    dt_t = _dt_proj(h1b, w_in_t, row0=dt_lo, n_heads=n_ssd_heads, tm=tm_mm)

    y_ssd = _ssd(zs, xbc, dt_t, conv_w, conv_b, dt_bias, a_log, d_skip, ssd_norm_g,
                 d_inner=d_inner, rows=min(SSD_ROWS_PER_STEP, t))
    y_attn = _swa(qkv, positions, attn_sinks, q_col=0, q_width=q_width,
                  k_col=q_width, v_col=q_width + kv_width, kv_width=kv_width)

    merged = _merge(y_ssd, y_attn, w_ssd_o, w_attn_o, gates, gs_col=0, ga_col=d, tm=tm_ffn, tn=512)
    h2 = _proj_ln(merged, w_out, h1, vec(ln2_g), vec(ln2_b), tm=tm_ffn)

    (out,) = _ffn_ln(h2, wg2, wu2, wd2, vec(ln3_g), vec(ln3_b), tm=tm_ffn, th=512, emit_bf16=False, name="ffn2_ln")
    return out


def kernel(x, positions, ffn1_w_gate, ffn1_w_up, ffn1_w_down, ln1_g, ln1_b, w_in, conv_w, conv_b, dt_bias, a_log, d_skip, ssd_norm_g, w_ssd_o, attn_sinks, w_attn_o, w_out, ln2_g, ln2_b, ffn2_w_gate, ffn2_w_up, ffn2_w_down, ln3_g, ln3_b):
    batch, depth = x.shape[0], ffn1_w_gate.shape[0]
    assert depth == DEPTH
    outs = []
    for bi in range(batch):
        h = x[bi]
        for l in range(depth):
            h = _layer(h, positions[bi], ffn1_w_gate[l], ffn1_w_up[l], ffn1_w_down[l], ln1_g[l], ln1_b[l],
                       w_in[l], conv_w[l], conv_b[l], dt_bias[l], a_log[l], d_skip[l], ssd_norm_g[l], w_ssd_o[l],
                       attn_sinks[l], w_attn_o[l], w_out[l], ln2_g[l], ln2_b[l],
                       ffn2_w_gate[l], ffn2_w_up[l], ffn2_w_down[l], ln3_g[l], ln3_b[l])
        outs.append(h)
    return jnp.stack(outs, axis=0)
```

```python
import functools
import math

import jax
import jax.numpy as jnp
from jax import lax
from jax.experimental import pallas as pl
from jax.experimental.pallas import tpu as pltpu

F32 = jnp.float32
BF16 = jnp.bfloat16

V7X_LANES = 128
V7X_SUBLANES = 8
V7X_VMEM_BYTES = 64 * 1024 * 1024

SSD_HEAD_DIM = 64
SSD_N_GROUPS = 8
SSD_D_STATE = 128
SSD_CONV_WIDTH = 4
SSD_CHUNK = 128
ATTN_HEAD_DIM = 64
ATTN_GROUP = 8
ATTN_WINDOW = 128
ROPE_THETA = 10000.0
DEPTH = 1
DEEPNORM_ALPHA = (2 * DEPTH) ** 0.25
LN_EPS = 1e-5
RMS_EPS = 1e-5
LOG2_E = math.log2(math.e)

CAST_ROWS = 256
SSD_ROWS_PER_STEP = 16 * SSD_CHUNK
FFN_HIDDEN_SLICES = 2
SWA_BLOCKS_PER_STEP = 1


def _vmem_limit(block_bytes):
    return int(min(block_bytes * 5 // 4 + (8 << 20), V7X_VMEM_BYTES - (6 << 20)))


def _dot(a, b):
    return jnp.dot(a, b, preferred_element_type=F32)


def _dot_nt(a, b):
    return lax.dot_general(a, b, (((1,), (1,)), ((), ())), preferred_element_type=F32)


def _dot_tn(a, b):
    return lax.dot_general(a, b, (((0,), (0,)), ((), ())), preferred_element_type=F32)


def _dot_f32(a, b):
    return jnp.dot(a, b, preferred_element_type=F32, precision=lax.Precision.HIGHEST)


def _layer_norm(y, g, b):
    mu = jnp.mean(y, axis=-1, keepdims=True)
    yc = y - mu
    var = jnp.mean(yc * yc, axis=-1, keepdims=True)
    return yc * lax.rsqrt(var + LN_EPS) * g + b


def _silu(x):
    return x * jax.nn.sigmoid(x)


def _cast_weight(dst_ref, w_ref):
    k = dst_ref.shape[0]
    for r0 in range(0, k, CAST_ROWS):
        rows = slice(r0, min(r0 + CAST_ROWS, k))
        dst_ref[rows, :] = w_ref[rows, :].astype(BF16)


def _ffn_ln_kernel(x_ref, wg_ref, wu_ref, wd_ref, g_ref, b_ref, *refs, n_hidden_tiles, emit_bf16, n_side):
    side_in, refs = refs[:n_side], refs[n_side:]
    if emit_bf16:
        o_ref, ob_ref = refs[:2]
        refs = refs[2:]
    else:
        o_ref = refs[0]
        refs = refs[1:]
    side_out, (xb_ref, acc_ref) = refs[:n_side], refs[n_side:]
    j = pl.program_id(1)

    for src_ref, dst_ref in zip(side_in, side_out):
        dst_ref[...] = src_ref[...].astype(BF16)

    @pl.when(j == 0)
    def _():
        xb_ref[...] = x_ref[...].astype(BF16)
        acc_ref[...] = jnp.zeros_like(acc_ref)

    xb = xb_ref[...]
    th = wg_ref.shape[1]
    partial = None
    for h0 in range(0, th, th // FFN_HIDDEN_SLICES):
        cols = slice(h0, h0 + th // FFN_HIDDEN_SLICES)
        gate = _dot(xb, wg_ref[:, cols])
        up = _dot(xb, wu_ref[:, cols])
        act = (_silu(gate) * up).astype(BF16)
        down = _dot(act, wd_ref[cols, :])
        partial = down if partial is None else partial + down
    acc_ref[...] += partial

    @pl.when(j == n_hidden_tiles - 1)
    def _():
        y = DEEPNORM_ALPHA * x_ref[...] + 0.5 * acc_ref[...]
        out = _layer_norm(y, g_ref[...], b_ref[...])
        o_ref[...] = out
        if emit_bf16:
            ob_ref[...] = out.astype(BF16)


def _ffn_ln(x, wg, wu, wd, g, b, *, tm, th, emit_bf16, name, side_cast=()):
    t, d = x.shape
    hidden = wg.shape[1]
    assert t % tm == 0 and hidden % th == 0
    n_i, n_j = t // tm, hidden // th
    row = pl.BlockSpec((tm, d), lambda i, j: (i, 0))
    vec = pl.BlockSpec((1, d), lambda i, j: (0, 0))
    out_shape = [jax.ShapeDtypeStruct((t, d), F32)]
    out_specs = [row]
    if emit_bf16:
        out_shape.append(jax.ShapeDtypeStruct((t, d), BF16))
        out_specs.append(row)
    block_bytes = (2 * tm * d * 4 + 2 * 3 * d * th * 2 + 2 * tm * d * 4 + (2 * tm * d * 2 if emit_bf16 else 0)
                   + tm * d * 2 + tm * d * 4)
    side_specs = []
    for w in side_cast:
        r, c = w.shape
        if r % n_i == 0 and c % n_j == 0 and (c // n_j) % V7X_LANES == 0:
            spec = pl.BlockSpec((r // n_i, c // n_j), lambda i, j: (i, j))
        else:
            assert r % n_j == 0 and c % n_i == 0 and (c // n_i) % V7X_LANES == 0
            spec = pl.BlockSpec((r // n_j, c // n_i), lambda i, j: (j, i))
        side_specs.append(spec)
        out_shape.append(jax.ShapeDtypeStruct((r, c), BF16))
        block_bytes += 2 * (r * c // (n_i * n_j)) * (4 + 2)
    return pl.pallas_call(
        functools.partial(_ffn_ln_kernel, n_hidden_tiles=n_j, emit_bf16=emit_bf16, n_side=len(side_cast)),
        grid=(n_i, n_j),
        in_specs=[row,
                  pl.BlockSpec((d, th), lambda i, j: (0, j)),
                  pl.BlockSpec((d, th), lambda i, j: (0, j)),
                  pl.BlockSpec((th, d), lambda i, j: (j, 0)),
                  vec, vec] + side_specs,
        out_specs=out_specs + side_specs,
        out_shape=out_shape,
        scratch_shapes=[pltpu.VMEM((tm, d), BF16), pltpu.VMEM((tm, d), F32)],
        compiler_params=pltpu.CompilerParams(
            dimension_semantics=("parallel", "arbitrary"), vmem_limit_bytes=_vmem_limit(block_bytes)),
        name=name,
    )(x, wg, wu, wd, g, b, *side_cast)


def _in_proj_kernel(a_ref, w_ref, o_ref, wb_ref, *, apply_silu):
    @pl.when(pl.program_id(1) == 0)
    def _():
        _cast_weight(wb_ref, w_ref)

    raw = _dot_nt(a_ref[...], wb_ref[...])
    o_ref[...] = _silu(raw) if apply_silu else raw


def _in_proj(a, w_t, *, row0, n, tm, tn, weight_buffers, name, apply_silu=False):
    m, k = a.shape
    assert m % tm == 0 and n % tn == 0 and row0 % V7X_SUBLANES == 0
    w_spec = pl.BlockSpec((pl.Element(tn), pl.Element(k)),
                          lambda j, i: (pl.multiple_of(row0 + j * tn, V7X_SUBLANES), 0),
                          pipeline_mode=pl.Buffered(weight_buffers))
    block_bytes = 2 * (tm * k * 2 + tm * tn * 4) + weight_buffers * tn * k * 4 + tn * k * 2
    return pl.pallas_call(
        functools.partial(_in_proj_kernel, apply_silu=apply_silu),
        grid=(n // tn, m // tm),
        in_specs=[pl.BlockSpec((tm, k), lambda j, i: (i, 0)), w_spec],
        out_specs=pl.BlockSpec((tm, tn), lambda j, i: (i, j)),
        out_shape=jax.ShapeDtypeStruct((m, n), F32),
        scratch_shapes=[pltpu.VMEM((tn, k), BF16)],
        compiler_params=pltpu.CompilerParams(
            dimension_semantics=("parallel", "arbitrary"), vmem_limit_bytes=_vmem_limit(block_bytes)),
        name=name,
    )(a, w_t)


def _dt_proj_kernel(a_ref, w_ref, o_ref):
    o_ref[...] = _dot_nt(w_ref[...].astype(BF16), a_ref[...])


def _dt_proj(a, w_t, *, row0, n_heads, tm):
    m, k = a.shape
    assert row0 % n_heads == 0 and n_heads % V7X_SUBLANES == 0 and m % tm == 0
    block_bytes = 2 * (tm * k * 2 + n_heads * k * 4 + n_heads * tm * 4)
    return pl.pallas_call(
        _dt_proj_kernel,
        grid=(m // tm,),
        in_specs=[pl.BlockSpec((tm, k), lambda i: (i, 0)),
                  pl.BlockSpec((n_heads, k), lambda i: (row0 // n_heads, 0))],
        out_specs=pl.BlockSpec((n_heads, tm), lambda i: (0, i)),
        out_shape=jax.ShapeDtypeStruct((n_heads, m), F32),
        compiler_params=pltpu.CompilerParams(
            dimension_semantics=("parallel",), vmem_limit_bytes=_vmem_limit(block_bytes)),
        name="dt_proj",
    )(a, w_t)


def _ssd_kernel(x_ref, b_ref, c_ref, z_ref, dt_ref,
                cwx_ref, cwb_ref, cwc_ref, cbx_ref, cbb_ref, cbc_ref,
                bias_ref, alog_ref, dskip_ref, ng_ref,
                y_ref, xpad_ref, bpad_ref, cpad_ref, xs_ref, bm_ref, cm_ref, state_ref):
    q = SSD_CHUNK
    tail = V7X_SUBLANES
    rows = x_ref.shape[0]
    gw = x_ref.shape[1]
    heads = gw // SSD_HEAD_DIM
    pairs = heads // 2
    pw = 2 * SSD_HEAD_DIM

    @pl.when(pl.program_id(1) == 0)
    def _():
        xpad_ref[0:tail, :] = jnp.zeros((tail, xpad_ref.shape[1]), F32)
        bpad_ref[0:tail, :] = jnp.zeros((tail, bpad_ref.shape[1]), F32)
        cpad_ref[0:tail, :] = jnp.zeros((tail, cpad_ref.shape[1]), F32)
        state_ref[...] = jnp.zeros_like(state_ref)

    def conv_silu(u_ref, pad_ref, w_ref, cbias_ref, dst_ref):
        pad_ref[tail:tail + rows, :] = u_ref[...]
        acc = cbias_ref[...]
        for k in range(SSD_CONV_WIDTH):
            start = tail - (SSD_CONV_WIDTH - 1) + k
            acc = acc + w_ref[k:k + 1, :] * pad_ref[start:start + rows, :]
        pad_ref[0:tail, :] = pad_ref[rows:rows + tail, :]
        dst_ref[...] = _silu(acc).astype(dst_ref.dtype)

    conv_silu(x_ref, xpad_ref, cwx_ref, cbx_ref, xs_ref)
    conv_silu(b_ref, bpad_ref, cwb_ref, cbb_ref, bm_ref)
    conv_silu(c_ref, cpad_ref, cwc_ref, cbc_ref, cm_ref)

    dt_all = jax.nn.softplus(dt_ref[...] + bias_ref[...])
    adt_all = dt_all * (-jnp.exp(alog_ref[...]) * LOG2_E)

    ri = lax.broadcasted_iota(jnp.int32, (q, q), 0)
    ci = lax.broadcasted_iota(jnp.int32, (q, q), 1)
    causal = ci <= ri
    upper = (ri <= ci).astype(F32)
    lo = lax.broadcasted_iota(jnp.int32, (q, pw), 1) < SSD_HEAD_DIM
    lo_row = lax.broadcasted_iota(jnp.int32, (1, pw), 1) < SSD_HEAD_DIM

    staged = []
    for ck in range(rows // q):
        r = slice(ck * q, (ck + 1) * q)
        dt_r = dt_all[:, r]
        cs_r = _dot_f32(adt_all[:, r], upper)

        def over_lanes(row0, row1):
            stacked = jnp.concatenate([jnp.broadcast_to(row0, (SSD_HEAD_DIM, q)),
                                       jnp.broadcast_to(row1, (SSD_HEAD_DIM, q))], axis=0)
            return stacked.T

        cs_l = [over_lanes(cs_r[h:h + 1, :], cs_r[h:h + 1, :]) for h in range(heads)]

        bm_b = bm_ref[r, :]
        cm_b = cm_ref[r, :]
        cb = _dot_nt(cm_b, bm_b)

        y_parts, din_parts, xdd_parts, cd_parts = [], [], [], []
        for p in range(pairs):
            h0, h1 = 2 * p, 2 * p + 1
            sl = slice(p * pw, (p + 1) * pw)
            xp = xs_ref[r, sl]
            xd = xp * over_lanes(dt_r[h0:h0 + 1, :], dt_r[h1:h1 + 1, :])
            xd_b = xd.astype(BF16)
            y_diag = []
            for h in (h0, h1):
                seg = jnp.where(causal, jnp.exp2(cs_l[h] - cs_r[h:h + 1, :]), 0.0)
                y_diag.append(_dot((cb * seg).astype(BF16), xd_b))
            cs_p = jnp.where(lo, cs_l[h0], cs_l[h1])
            cs_end = jnp.where(lo_row, cs_l[h0][q - 1:q, :], cs_l[h1][q - 1:q, :])
            y_parts.append(jnp.where(lo, y_diag[0], y_diag[1]) + dskip_ref[:, sl] * xp)
            din_parts.append(jnp.exp2(cs_p))
            xdd_parts.append((xd * jnp.exp2(cs_end - cs_p)).astype(BF16))
            cd_parts.append(jnp.exp2(cs_end))
        staged.append((cm_b, bm_b, jnp.concatenate(y_parts, axis=1), jnp.concatenate(din_parts, axis=1),
                       jnp.concatenate(xdd_parts, axis=1), jnp.concatenate(cd_parts, axis=1)))

    for ck, (cm_b, bm_b, y_local, decay_in, xdd, chunk_decay) in enumerate(staged):
        r = slice(ck * q, (ck + 1) * q)
        state = state_ref[...]
        y = y_local + _dot(cm_b, state.astype(BF16)) * decay_in
        state_ref[...] = state * chunk_decay + _dot_tn(bm_b, xdd)
        y = y * z_ref[r, :]
        y = y * lax.rsqrt(jnp.mean(y * y, axis=-1, keepdims=True) + RMS_EPS) * ng_ref[...]
        y_ref[r, :] = y.astype(y_ref.dtype)


def _ssd(zs, xbc, dt_t, conv_w, conv_b, dt_bias, a_log, d_skip, norm_g, *, d_inner, rows):
    t = zs.shape[0]
    g = SSD_N_GROUPS
    gw = d_inner // g
    hg = gw // SSD_HEAD_DIM
    n = SSD_D_STATE
    assert t % rows == 0 and rows % SSD_CHUNK == 0 and d_inner % n == 0
    assert 2 * hg <= SSD_CHUNK and hg % V7X_SUBLANES == 0
    conv_b2 = conv_b.reshape(1, -1)
    per_head = lambda v: v.reshape(g * hg, 1)
    dskip_l = jnp.repeat(d_skip, SSD_HEAD_DIM).reshape(1, d_inner)
    ng = norm_g.reshape(1, d_inner)
    b_blk = d_inner // n
    c_blk = b_blk + g

    head_col = pl.BlockSpec((hg, 1), lambda gi, ci: (gi, 0))
    in_specs = [
        pl.BlockSpec((rows, gw), lambda gi, ci: (ci, gi)),
        pl.BlockSpec((rows, n), lambda gi, ci: (ci, b_blk + gi)),
        pl.BlockSpec((rows, n), lambda gi, ci: (ci, c_blk + gi)),
        pl.BlockSpec((rows, gw), lambda gi, ci: (ci, gi)),
        pl.BlockSpec((hg, rows), lambda gi, ci: (gi, ci)),
        pl.BlockSpec((SSD_CONV_WIDTH, gw), lambda gi, ci: (0, gi)),
        pl.BlockSpec((SSD_CONV_WIDTH, n), lambda gi, ci: (0, b_blk + gi)),
        pl.BlockSpec((SSD_CONV_WIDTH, n), lambda gi, ci: (0, c_blk + gi)),
        pl.BlockSpec((1, gw), lambda gi, ci: (0, gi)),
        pl.BlockSpec((1, n), lambda gi, ci: (0, b_blk + gi)),
        pl.BlockSpec((1, n), lambda gi, ci: (0, c_blk + gi)),
        head_col, head_col,
        pl.BlockSpec((1, gw), lambda gi, ci: (0, gi)),
        pl.BlockSpec((1, gw), lambda gi, ci: (0, gi)),
    ]
    pad_rows = rows + V7X_SUBLANES
    block_bytes = (2 * (2 * rows * gw * 4 + 2 * rows * n * 4 + rows * gw * 2) + pad_rows * (gw + 2 * n) * 4
                   + rows * gw * 4 + 2 * rows * n * 2 + n * gw * 4)
    return pl.pallas_call(
        _ssd_kernel,
        grid=(g, t // rows),
        in_specs=in_specs,
        out_specs=pl.BlockSpec((rows, gw), lambda gi, ci: (ci, gi)),
        out_shape=jax.ShapeDtypeStruct((t, d_inner), BF16),
        scratch_shapes=[pltpu.VMEM((pad_rows, gw), F32), pltpu.VMEM((pad_rows, n), F32),
                        pltpu.VMEM((pad_rows, n), F32), pltpu.VMEM((rows, gw), F32),
                        pltpu.VMEM((rows, n), BF16), pltpu.VMEM((rows, n), BF16),
                        pltpu.VMEM((n, gw), F32)],
        compiler_params=pltpu.CompilerParams(
            dimension_semantics=("parallel", "arbitrary"), vmem_limit_bytes=_vmem_limit(block_bytes)),
        name="ssd",
    )(xbc, xbc, xbc, zs, dt_t, conv_w, conv_w, conv_w, conv_b2, conv_b2, conv_b2,
      per_head(dt_bias), per_head(a_log), dskip_l, ng)


def _swa_kernel(q_ref, k_ref, v_ref, pos_ref, invf_ref, sink_ref, o_ref, kprev_ref, vprev_ref):
    w = ATTN_WINDOW
    dh = ATTN_HEAD_DIM
    half = dh // 2
    lanes = 2 * dh
    n_kv = k_ref.shape[1] // dh
    tiles_per_kv = ATTN_GROUP // 2
    nb = pl.program_id(0)

    @pl.when(nb == 0)
    def _():
        kprev_ref[...] = jnp.zeros_like(kprev_ref)
        vprev_ref[...] = jnp.zeros_like(vprev_ref)

    lane = lax.broadcasted_iota(jnp.int32, (w, lanes), 1)
    first_half = (lane % dh) < half
    lo2 = lax.broadcasted_iota(jnp.int32, (2 * w, lanes), 1) < dh

    def attend(rb, prev_ok):
        ang = pos_ref[rb, :].astype(F32) * invf_ref[...]
        cos = jnp.cos(ang)
        sin = jnp.sin(ang)
        sin = jnp.where(first_half, -sin, sin)
        scale = dh ** -0.5

        def rope(u, c, s):
            partner = jnp.where(first_half, pltpu.roll(u, lanes - half, 1), pltpu.roll(u, half, 1))
            return u * c + partner * s

        kj = lax.broadcasted_iota(jnp.int32, (2 * w, w), 0)
        qi = lax.broadcasted_iota(jnp.int32, (2 * w, w), 1)
        valid = (kj > qi) & (kj <= qi + w) & ((kj >= w) | prev_ok)
        bias = jnp.where(valid, 0.0, -jnp.inf)

        cos_q, sin_q = cos * (scale * LOG2_E), sin * (scale * LOG2_E)
        q_tiles = [rope(q_ref[rb, t * lanes:(t + 1) * lanes], cos_q, sin_q).astype(BF16)
                   for t in range(q_ref.shape[1] // lanes)]

        units = []
        for pt in range(n_kv // 2):
            sl = slice(pt * lanes, (pt + 1) * lanes)
            k_cur = rope(k_ref[rb, sl], cos, sin)
            v_cur = v_ref[rb, sl]
            k_pair = jnp.concatenate([kprev_ref[:, sl], k_cur], axis=0)
            v_pair = jnp.concatenate([vprev_ref[:, sl], v_cur], axis=0)
            kprev_ref[:, sl] = k_cur
            vprev_ref[:, sl] = v_cur
            k_swap = pltpu.roll(k_pair, dh, 1)
            vt_pair = v_pair.T
            for side in range(2):
                hk = 2 * pt + side
                k_lo = jnp.where(lo2, k_pair if side == 0 else k_swap, 0.0).astype(BF16)
                k_hi = jnp.where(lo2, 0.0, k_swap if side == 0 else k_pair).astype(BF16)
                vt = vt_pair[side * dh:(side + 1) * dh, :].astype(BF16)
                units += [(hk, 0, k_lo, vt), (hk, 1, k_hi, vt)]

        scores = []
        for hk, parity, k_sel, vt in units:
            qs = jnp.concatenate(q_tiles[hk * tiles_per_kv:(hk + 1) * tiles_per_kv], axis=0)
            scores.append(_dot_nt(k_sel, qs))

        probs = []
        for (hk, parity, k_sel, vt), st in zip(units, scores):
            p_parts, den_parts = [], []
            for j in range(tiles_per_kv):
                head = hk * ATTN_GROUP + 2 * j + parity
                s = st[:, j * w:(j + 1) * w] + bias
                sink = sink_ref[0:1, head:head + 1] * LOG2_E
                m = jnp.maximum(jnp.max(s, axis=0, keepdims=True), sink)
                p = jnp.exp2(s - m)
                den_parts.append(jnp.sum(p, axis=0, keepdims=True) + jnp.exp2(sink - m))
                p_parts.append(p.astype(BF16))
            probs.append((jnp.concatenate(p_parts, axis=1), jnp.concatenate(den_parts, axis=1)))

        outs = {}
        for (hk, parity, k_sel, vt), (pt_all, den) in zip(units, probs):
            outs[hk, parity] = _dot(vt, pt_all) / den

        for hk in range(n_kv):
            for j in range(tiles_per_kv):
                qt = hk * tiles_per_kv + j
                tile_t = jnp.concatenate([outs[hk, 0][:, j * w:(j + 1) * w],
                                          outs[hk, 1][:, j * w:(j + 1) * w]], axis=0)
                o_ref[rb, qt * lanes:(qt + 1) * lanes] = tile_t.T.astype(o_ref.dtype)

    for blk in range(q_ref.shape[0] // w):
        attend(slice(blk * w, (blk + 1) * w), (nb > 0) if blk == 0 else True)


def _swa(proj, positions, sinks, *, q_col, q_width, k_col, v_col, kv_width):
    t = proj.shape[0]
    w = ATTN_WINDOW
    assert t % w == 0 and q_col % q_width == 0 and k_col % kv_width == 0 and v_col % kv_width == 0
    half = ATTN_HEAD_DIM // 2
    inv_freq = ROPE_THETA ** (-jnp.arange(half, dtype=F32) * 2.0 / ATTN_HEAD_DIM)
    inv_freq = jnp.tile(inv_freq, 2 * V7X_LANES // ATTN_HEAD_DIM).reshape(1, V7X_LANES)
    rows = min(SWA_BLOCKS_PER_STEP * w, t)
    assert t % rows == 0
    block_bytes = 2 * (rows * q_width * 4 + 2 * rows * kv_width * 4 + rows * q_width * 2) + 2 * w * kv_width * 4
    return pl.pallas_call(
        _swa_kernel,
        grid=(t // rows,),
        in_specs=[
            pl.BlockSpec((rows, q_width), lambda i: (i, q_col // q_width)),
            pl.BlockSpec((rows, kv_width), lambda i: (i, k_col // kv_width)),
            pl.BlockSpec((rows, kv_width), lambda i: (i, v_col // kv_width)),
            pl.BlockSpec((rows, 1), lambda i: (i, 0)),
            pl.BlockSpec((1, V7X_LANES), lambda i: (0, 0)),
            pl.BlockSpec((1, sinks.shape[0]), lambda i: (0, 0)),
        ],
        out_specs=pl.BlockSpec((rows, q_width), lambda i: (i, 0)),
        out_shape=jax.ShapeDtypeStruct((t, q_width), BF16),
        scratch_shapes=[pltpu.VMEM((w, kv_width), F32), pltpu.VMEM((w, kv_width), F32)],
        compiler_params=pltpu.CompilerParams(
            dimension_semantics=("arbitrary",), vmem_limit_bytes=_vmem_limit(block_bytes)),
        name="swa",
    )(proj, proj, proj, positions.reshape(t, 1), inv_freq, sinks.reshape(1, -1))


def _merge_kernel(ys_ref, ya_ref, ws_ref, wa_ref, gs_ref, ga_ref, o_ref, wsb_ref, wab_ref):
    @pl.when(pl.program_id(1) == 0)
    def _():
        _cast_weight(wsb_ref, ws_ref)
        _cast_weight(wab_ref, wa_ref)

    tn = o_ref.shape[1]
    for c0 in range(0, tn, tn // 2):
        cols = slice(c0, c0 + tn // 2)
        y_s = _dot(ys_ref[...], wsb_ref[:, cols])
        y_a = _dot(ya_ref[...], wab_ref[:, cols])
        merged = jax.nn.sigmoid(gs_ref[:, cols]) * y_s + jax.nn.sigmoid(ga_ref[:, cols]) * y_a
        o_ref[:, cols] = merged.astype(o_ref.dtype)


def _merge(y_ssd, y_attn, w_ssd_o, w_attn_o, proj, *, gs_col, ga_col, tm, tn):
    t, ks = y_ssd.shape
    ka = y_attn.shape[1]
    d = w_ssd_o.shape[1]
    assert t % tm == 0 and d % tn == 0 and gs_col % tn == 0 and ga_col % tn == 0
    block_bytes = (2 * (tm * ks * 2 + tm * ka * 2 + ks * tn * 4 + ka * tn * 4 + 2 * tm * tn * 4 + tm * tn * 2)
                   + (ks + ka) * tn * 2)
    return pl.pallas_call(
        _merge_kernel,
        grid=(d // tn, t // tm),
        in_specs=[pl.BlockSpec((tm, ks), lambda j, i: (i, 0)),
                  pl.BlockSpec((tm, ka), lambda j, i: (i, 0)),
                  pl.BlockSpec((ks, tn), lambda j, i: (0, j)),
                  pl.BlockSpec((ka, tn), lambda j, i: (0, j)),
                  pl.BlockSpec((tm, tn), lambda j, i: (i, gs_col // tn + j)),
                  pl.BlockSpec((tm, tn), lambda j, i: (i, ga_col // tn + j))],
        out_specs=pl.BlockSpec((tm, tn), lambda j, i: (i, j)),
        out_shape=jax.ShapeDtypeStruct((t, d), BF16),
        scratch_shapes=[pltpu.VMEM((ks, tn), BF16), pltpu.VMEM((ka, tn), BF16)],
        compiler_params=pltpu.CompilerParams(
            dimension_semantics=("parallel", "arbitrary"), vmem_limit_bytes=_vmem_limit(block_bytes)),
        name="merge",
    )(y_ssd, y_attn, w_ssd_o, w_attn_o, proj, proj)


def _proj_ln_kernel(m_ref, w_ref, h_ref, g_ref, b_ref, o_ref, wb_ref):
    @pl.when(pl.program_id(0) == 0)
    def _():
        _cast_weight(wb_ref, w_ref)

    y = DEEPNORM_ALPHA * h_ref[...] + _dot(m_ref[...], wb_ref[...])
    o_ref[...] = _layer_norm(y, g_ref[...], b_ref[...])


def _proj_ln(merged, w_out, h, g, b, *, tm):
    t, d = h.shape
    k = merged.shape[1]
    assert t % tm == 0
    vec = pl.BlockSpec((1, d), lambda i: (0, 0))
    block_bytes = 2 * (tm * k * 2 + 2 * tm * d * 4) + k * d * 4 + k * d * 2
    return pl.pallas_call(
        _proj_ln_kernel,
        grid=(t // tm,),
        in_specs=[pl.BlockSpec((tm, k), lambda i: (i, 0)),
                  pl.BlockSpec((k, d), lambda i: (0, 0), pipeline_mode=pl.Buffered(1)),
                  pl.BlockSpec((tm, d), lambda i: (i, 0)),
                  vec, vec],
        out_specs=pl.BlockSpec((tm, d), lambda i: (i, 0)),
        out_shape=jax.ShapeDtypeStruct((t, d), F32),
        scratch_shapes=[pltpu.VMEM((k, d), BF16)],
        compiler_params=pltpu.CompilerParams(
            dimension_semantics=("arbitrary",), vmem_limit_bytes=_vmem_limit(block_bytes)),
        name="proj_ln",
    )(merged, w_out, h, g, b)


def _layer(x, positions, ffn1_w_gate, ffn1_w_up, ffn1_w_down, ln1_g, ln1_b,
           w_in, conv_w, conv_b, dt_bias, a_log, d_skip, ssd_norm_g, w_ssd_o,
           attn_sinks, w_attn_o, w_out, ln2_g, ln2_b,
           ffn2_w_gate, ffn2_w_up, ffn2_w_down, ln3_g, ln3_b):
    t, d = x.shape
    d_inner = w_ssd_o.shape[0]
    n_ssd_heads = dt_bias.shape[0]
    q_width = w_attn_o.shape[0]
    kv_width = q_width // ATTN_GROUP
    xbc_width = d_inner + 2 * SSD_N_GROUPS * SSD_D_STATE
    vec = lambda v: v.reshape(1, -1)
    tm_ffn = min(512, t)
    tm_mm = min(1024, t)

    sizes = (d_inner, xbc_width, n_ssd_heads, q_width, kv_width, kv_width, d, d)
    starts = [0]
    for s in sizes:
        starts.append(starts[-1] + s)
    dt_lo, dt_hi, total = starts[2], starts[3], starts[-1]

    h1, h1b, wg2, wu2, wd2 = _ffn_ln(
        x, ffn1_w_gate.astype(BF16), ffn1_w_up.astype(BF16), ffn1_w_down.astype(BF16), vec(ln1_g), vec(ln1_b),
        tm=tm_ffn, th=512, emit_bf16=True, name="ffn1_ln", side_cast=(ffn2_w_gate, ffn2_w_up, ffn2_w_down))

    w_in_t = w_in.T
    n_qkv = q_width + 2 * kv_width
    zs = _in_proj(h1b, w_in_t, row0=0, n=d_inner, tm=tm_mm, tn=1024, weight_buffers=2, name="in_proj_z",
                  apply_silu=True)
    xbc = _in_proj(h1b, w_in_t, row0=d_inner, n=xbc_width, tm=tm_mm, tn=1024, weight_buffers=2,
                   name="in_proj_xbc")
    qkv = _in_proj(h1b, w_in_t, row0=dt_hi, n=n_qkv, tm=tm_mm, tn=n_qkv // 2, weight_buffers=2,
                   name="in_proj_qkv")
    gates = _in_proj(h1b, w_in_t, row0=dt_hi + n_qkv, n=2 * d, tm=tm_mm, tn=1024, weight_buffers=2,
                     name="in_proj_gates")
    dt_t = _dt_proj(h1b, w_in_t, row0=dt_lo, n_heads=n_ssd_heads, tm=tm_mm)

    y_ssd = _ssd(zs, xbc, dt_t, conv_w, conv_b, dt_bias, a_log, d_skip, ssd_norm_g,
                 d_inner=d_inner, rows=min(SSD_ROWS_PER_STEP, t))
    y_attn = _swa(qkv, positions, attn_sinks, q_col=0, q_width=q_width,
                  k_col=q_width, v_col=q_width + kv_width, kv_width=kv_width)

    merged = _merge(y_ssd, y_attn, w_ssd_o, w_attn_o, gates, gs_col=0, ga_col=d, tm=tm_ffn, tn=512)
    h2 = _proj_ln(merged, w_out, h1, vec(ln2_g), vec(ln2_b), tm=tm_ffn)

    (out,) = _ffn_ln(h2, wg2, wu2, wd2, vec(ln3_g), vec(ln3_b), tm=tm_ffn, th=512, emit_bf16=False, name="ffn2_ln")
    return out


def kernel(x, positions, ffn1_w_gate, ffn1_w_up, ffn1_w_down, ln1_g, ln1_b, w_in, conv_w, conv_b, dt_bias, a_log, d_skip, ssd_norm_g, w_ssd_o, attn_sinks, w_attn_o, w_out, ln2_g, ln2_b, ffn2_w_gate, ffn2_w_up, ffn2_w_down, ln3_g, ln3_b):
    batch, depth = x.shape[0], ffn1_w_gate.shape[0]
    assert depth == DEPTH
    outs = []
    for bi in range(batch):
        h = x[bi]
        for l in range(depth):
            h = _layer(h, positions[bi], ffn1_w_gate[l], ffn1_w_up[l], ffn1_w_down[l], ln1_g[l], ln1_b[l],
                       w_in[l], conv_w[l], conv_b[l], dt_bias[l], a_log[l], d_skip[l], ssd_norm_g[l], w_ssd_o[l],
                       attn_sinks[l], w_attn_o[l], w_out[l], ln2_g[l], ln2_b[l],
                       ffn2_w_gate[l], ffn2_w_up[l], ffn2_w_down[l], ln3_g[l], ln3_b[l])
        outs.append(h)
    return jnp.stack(outs, axis=0)
```

```python
import functools
import math

import jax
import jax.numpy as jnp
from jax import lax
from jax.experimental import pallas as pl
from jax.experimental.pallas import tpu as pltpu

F32 = jnp.float32
BF16 = jnp.bfloat16

V7X_LANES = 128
V7X_SUBLANES = 8
V7X_VMEM_BYTES = 64 * 1024 * 1024

SSD_HEAD_DIM = 64
SSD_N_GROUPS = 8
SSD_D_STATE = 128
SSD_CONV_WIDTH = 4
SSD_CHUNK = 128
ATTN_HEAD_DIM = 64
ATTN_GROUP = 8
ATTN_WINDOW = 128
ROPE_THETA = 10000.0
DEPTH = 1
DEEPNORM_ALPHA = (2 * DEPTH) ** 0.25
LN_EPS = 1e-5
RMS_EPS = 1e-5
LOG2_E = math.log2(math.e)

CAST_ROWS = 256
SSD_ROWS_PER_STEP = 16 * SSD_CHUNK
FFN_HIDDEN_SLICES = 2
SWA_BLOCKS_PER_STEP = 1
IN_PROJ_RING_SLOTS = 3


def _vmem_limit(block_bytes):
    return int(min(block_bytes * 5 // 4 + (8 << 20), V7X_VMEM_BYTES - (6 << 20)))


def _dot(a, b):
    return jnp.dot(a, b, preferred_element_type=F32)


def _dot_nt(a, b):
    return lax.dot_general(a, b, (((1,), (1,)), ((), ())), preferred_element_type=F32)


def _dot_tn(a, b):
    return lax.dot_general(a, b, (((0,), (0,)), ((), ())), preferred_element_type=F32)


def _dot_f32(a, b):
    return jnp.dot(a, b, preferred_element_type=F32, precision=lax.Precision.HIGHEST)


def _layer_norm(y, g, b):
    mu = jnp.mean(y, axis=-1, keepdims=True)
    yc = y - mu
    var = jnp.mean(yc * yc, axis=-1, keepdims=True)
    return yc * lax.rsqrt(var + LN_EPS) * g + b


def _silu(x):
    return x * jax.nn.sigmoid(x)


def _cast_weight(dst_ref, w_ref):
    k = dst_ref.shape[0]
    for r0 in range(0, k, CAST_ROWS):
        rows = slice(r0, min(r0 + CAST_ROWS, k))
        dst_ref[rows, :] = w_ref[rows, :].astype(BF16)


def _ffn_ln_kernel(x_ref, wg_ref, wu_ref, wd_ref, g_ref, b_ref, *refs, n_hidden_tiles, emit_bf16, n_side):
    side_in, refs = refs[:n_side], refs[n_side:]
    if emit_bf16:
        o_ref, ob_ref = refs[:2]
        refs = refs[2:]
    else:
        o_ref = refs[0]
        refs = refs[1:]
    side_out, (xb_ref, acc_ref) = refs[:n_side], refs[n_side:]
    j = pl.program_id(1)

    for src_ref, dst_ref in zip(side_in, side_out):
        dst_ref[...] = src_ref[...].astype(BF16)

    @pl.when(j == 0)
    def _():
        xb_ref[...] = x_ref[...].astype(BF16)
        acc_ref[...] = jnp.zeros_like(acc_ref)

    xb = xb_ref[...]
    th = wg_ref.shape[1]
    partial = None
    for h0 in range(0, th, th // FFN_HIDDEN_SLICES):
        cols = slice(h0, h0 + th // FFN_HIDDEN_SLICES)
        gate = _dot(xb, wg_ref[:, cols])
        up = _dot(xb, wu_ref[:, cols])
        act = (_silu(gate) * up).astype(BF16)
        down = _dot(act, wd_ref[cols, :])
        partial = down if partial is None else partial + down
    acc_ref[...] += partial

    @pl.when(j == n_hidden_tiles - 1)
    def _():
        y = DEEPNORM_ALPHA * x_ref[...] + 0.5 * acc_ref[...]
        out = _layer_norm(y, g_ref[...], b_ref[...])
        o_ref[...] = out
        if emit_bf16:
            ob_ref[...] = out.astype(BF16)


def _ffn_ln(x, wg, wu, wd, g, b, *, tm, th, emit_bf16, name, side_cast=()):
    t, d = x.shape
    hidden = wg.shape[1]
    assert t % tm == 0 and hidden % th == 0
    n_i, n_j = t // tm, hidden // th
    row = pl.BlockSpec((tm, d), lambda i, j: (i, 0))
    vec = pl.BlockSpec((1, d), lambda i, j: (0, 0))
    out_shape = [jax.ShapeDtypeStruct((t, d), F32)]
    out_specs = [row]
    if emit_bf16:
        out_shape.append(jax.ShapeDtypeStruct((t, d), BF16))
        out_specs.append(row)
    block_bytes = (2 * tm * d * 4 + 2 * 3 * d * th * 2 + 2 * tm * d * 4 + (2 * tm * d * 2 if emit_bf16 else 0)
                   + tm * d * 2 + tm * d * 4)
    side_specs = []
    for w in side_cast:
        r, c = w.shape
        if r % n_i == 0 and c % n_j == 0 and (c // n_j) % V7X_LANES == 0:
            spec = pl.BlockSpec((r // n_i, c // n_j), lambda i, j: (i, j))
        else:
            assert r % n_j == 0 and c % n_i == 0 and (c // n_i) % V7X_LANES == 0
            spec = pl.BlockSpec((r // n_j, c // n_i), lambda i, j: (j, i))
        side_specs.append(spec)
        out_shape.append(jax.ShapeDtypeStruct((r, c), BF16))
        block_bytes += 2 * (r * c // (n_i * n_j)) * (4 + 2)
    return pl.pallas_call(
        functools.partial(_ffn_ln_kernel, n_hidden_tiles=n_j, emit_bf16=emit_bf16, n_side=len(side_cast)),
        grid=(n_i, n_j),
        in_specs=[row,
                  pl.BlockSpec((d, th), lambda i, j: (0, j)),
                  pl.BlockSpec((d, th), lambda i, j: (0, j)),
                  pl.BlockSpec((th, d), lambda i, j: (j, 0)),
                  vec, vec] + side_specs,
        out_specs=out_specs + side_specs,
        out_shape=out_shape,
        scratch_shapes=[pltpu.VMEM((tm, d), BF16), pltpu.VMEM((tm, d), F32)],
        compiler_params=pltpu.CompilerParams(
            dimension_semantics=("parallel", "arbitrary"), vmem_limit_bytes=_vmem_limit(block_bytes)),
        name=name,
    )(x, wg, wu, wd, g, b, *side_cast)


def _in_proj_kernel(a_hbm, w_ref, o_ref, wb_ref, abuf_ref, sem_ref, *, apply_silu):
    tm = abuf_ref.shape[1]
    n_i = pl.num_programs(1)
    total = pl.num_programs(0) * n_i
    step = pl.program_id(0) * n_i + pl.program_id(1)

    def row_tile_copy(s):
        row = pl.multiple_of(lax.rem(s, n_i) * tm, tm)
        slot = lax.rem(s, IN_PROJ_RING_SLOTS)
        return pltpu.make_async_copy(a_hbm.at[pl.ds(row, tm), :], abuf_ref.at[slot], sem_ref.at[slot])

    @pl.when(step == 0)
    def _():
        for s in range(IN_PROJ_RING_SLOTS - 1):
            row_tile_copy(s).start()

    @pl.when(step + IN_PROJ_RING_SLOTS - 1 < total)
    def _():
        row_tile_copy(step + IN_PROJ_RING_SLOTS - 1).start()

    @pl.when(pl.program_id(1) == 0)
    def _():
        _cast_weight(wb_ref, w_ref)

    row_tile_copy(step).wait()
    raw = _dot_nt(abuf_ref[lax.rem(step, IN_PROJ_RING_SLOTS)], wb_ref[...])
    o_ref[...] = _silu(raw) if apply_silu else raw


def _in_proj(a, w_t, *, row0, n, tm, tn, weight_buffers, name, apply_silu=False):
    m, k = a.shape
    assert m % tm == 0 and n % tn == 0 and row0 % V7X_SUBLANES == 0
    w_spec = pl.BlockSpec((pl.Element(tn), pl.Element(k)),
                          lambda j, i: (pl.multiple_of(row0 + j * tn, V7X_SUBLANES), 0),
                          pipeline_mode=pl.Buffered(weight_buffers))
    assert (n // tn) * (m // tm) >= IN_PROJ_RING_SLOTS - 1
    block_bytes = (IN_PROJ_RING_SLOTS * tm * k * 2 + 2 * tm * tn * 4 + weight_buffers * tn * k * 4 + tn * k * 2)
    return pl.pallas_call(
        functools.partial(_in_proj_kernel, apply_silu=apply_silu),
        grid=(n // tn, m // tm),
        in_specs=[pl.BlockSpec(memory_space=pl.ANY), w_spec],
        out_specs=pl.BlockSpec((tm, tn), lambda j, i: (i, j)),
        out_shape=jax.ShapeDtypeStruct((m, n), F32),
        scratch_shapes=[pltpu.VMEM((tn, k), BF16), pltpu.VMEM((IN_PROJ_RING_SLOTS, tm, k), BF16),
                        pltpu.SemaphoreType.DMA((IN_PROJ_RING_SLOTS,))],
        compiler_params=pltpu.CompilerParams(
            dimension_semantics=("arbitrary", "arbitrary"), vmem_limit_bytes=_vmem_limit(block_bytes)),
        name=name,
    )(a, w_t)


def _dt_proj_kernel(a_ref, w_ref, o_ref):
    o_ref[...] = _dot_nt(w_ref[...].astype(BF16), a_ref[...])


def _dt_proj(a, w_t, *, row0, n_heads, tm):
    m, k = a.shape
    assert row0 % n_heads == 0 and n_heads % V7X_SUBLANES == 0 and m % tm == 0
    block_bytes = 2 * (tm * k * 2 + n_heads * k * 4 + n_heads * tm * 4)
    return pl.pallas_call(
        _dt_proj_kernel,
        grid=(m // tm,),
        in_specs=[pl.BlockSpec((tm, k), lambda i: (i, 0)),
                  pl.BlockSpec((n_heads, k), lambda i: (row0 // n_heads, 0))],
        out_specs=pl.BlockSpec((n_heads, tm), lambda i: (0, i)),
        out_shape=jax.ShapeDtypeStruct((n_heads, m), F32),
        compiler_params=pltpu.CompilerParams(
            dimension_semantics=("parallel",), vmem_limit_bytes=_vmem_limit(block_bytes)),
        name="dt_proj",
    )(a, w_t)


def _ssd_kernel(x_ref, b_ref, c_ref, z_ref, dt_ref,
                cwx_ref, cwb_ref, cwc_ref, cbx_ref, cbb_ref, cbc_ref,
                bias_ref, alog_ref, dskip_ref, ng_ref,
                y_ref, xpad_ref, bpad_ref, cpad_ref, xs_ref, bm_ref, cm_ref, state_ref):
    q = SSD_CHUNK
    tail = V7X_SUBLANES
    rows = x_ref.shape[0]
    gw = x_ref.shape[1]
    heads = gw // SSD_HEAD_DIM
    pairs = heads // 2
    pw = 2 * SSD_HEAD_DIM

    @pl.when(pl.program_id(1) == 0)
    def _():
        xpad_ref[0:tail, :] = jnp.zeros((tail, xpad_ref.shape[1]), F32)
        bpad_ref[0:tail, :] = jnp.zeros((tail, bpad_ref.shape[1]), F32)
        cpad_ref[0:tail, :] = jnp.zeros((tail, cpad_ref.shape[1]), F32)
        state_ref[...] = jnp.zeros_like(state_ref)

    def conv_silu(u_ref, pad_ref, w_ref, cbias_ref, dst_ref):
        pad_ref[tail:tail + rows, :] = u_ref[...]
        acc = cbias_ref[...]
        for k in range(SSD_CONV_WIDTH):
            start = tail - (SSD_CONV_WIDTH - 1) + k
            acc = acc + w_ref[k:k + 1, :] * pad_ref[start:start + rows, :]
        pad_ref[0:tail, :] = pad_ref[rows:rows + tail, :]
        dst_ref[...] = _silu(acc).astype(dst_ref.dtype)

    conv_silu(x_ref, xpad_ref, cwx_ref, cbx_ref, xs_ref)
    conv_silu(b_ref, bpad_ref, cwb_ref, cbb_ref, bm_ref)
    conv_silu(c_ref, cpad_ref, cwc_ref, cbc_ref, cm_ref)

    dt_all = jax.nn.softplus(dt_ref[...] + bias_ref[...])
    adt_all = dt_all * (-jnp.exp(alog_ref[...]) * LOG2_E)

    ri = lax.broadcasted_iota(jnp.int32, (q, q), 0)
    ci = lax.broadcasted_iota(jnp.int32, (q, q), 1)
    causal = ci <= ri
    upper = (ri <= ci).astype(F32)
    lo = lax.broadcasted_iota(jnp.int32, (q, pw), 1) < SSD_HEAD_DIM
    lo_row = lax.broadcasted_iota(jnp.int32, (1, pw), 1) < SSD_HEAD_DIM

    staged = []
    for ck in range(rows // q):
        r = slice(ck * q, (ck + 1) * q)
        dt_r = dt_all[:, r]
        cs_r = _dot_f32(adt_all[:, r], upper)

        def over_lanes(row0, row1):
            stacked = jnp.concatenate([jnp.broadcast_to(row0, (SSD_HEAD_DIM, q)),
                                       jnp.broadcast_to(row1, (SSD_HEAD_DIM, q))], axis=0)
            return stacked.T

        cs_l = [over_lanes(cs_r[h:h + 1, :], cs_r[h:h + 1, :]) for h in range(heads)]

        bm_b = bm_ref[r, :]
        cm_b = cm_ref[r, :]
        cb = _dot_nt(cm_b, bm_b)

        y_parts, din_parts, xdd_parts, cd_parts = [], [], [], []
        for p in range(pairs):
            h0, h1 = 2 * p, 2 * p + 1
            sl = slice(p * pw, (p + 1) * pw)
            xp = xs_ref[r, sl]
            xd = xp * over_lanes(dt_r[h0:h0 + 1, :], dt_r[h1:h1 + 1, :])
            xd_b = xd.astype(BF16)
            y_diag = []
            for h in (h0, h1):
                seg = jnp.where(causal, jnp.exp2(cs_l[h] - cs_r[h:h + 1, :]), 0.0)
                y_diag.append(_dot((cb * seg).astype(BF16), xd_b))
            cs_p = jnp.where(lo, cs_l[h0], cs_l[h1])
            cs_end = jnp.where(lo_row, cs_l[h0][q - 1:q, :], cs_l[h1][q - 1:q, :])
            y_parts.append(jnp.where(lo, y_diag[0], y_diag[1]) + dskip_ref[:, sl] * xp)
            din_parts.append(jnp.exp2(cs_p))
            xdd_parts.append((xd * jnp.exp2(cs_end - cs_p)).astype(BF16))
            cd_parts.append(jnp.exp2(cs_end))
        staged.append((cm_b, bm_b, jnp.concatenate(y_parts, axis=1), jnp.concatenate(din_parts, axis=1),
                       jnp.concatenate(xdd_parts, axis=1), jnp.concatenate(cd_parts, axis=1)))

    for ck, (cm_b, bm_b, y_local, decay_in, xdd, chunk_decay) in enumerate(staged):
        r = slice(ck * q, (ck + 1) * q)
        state = state_ref[...]
        y = y_local + _dot(cm_b, state.astype(BF16)) * decay_in
        state_ref[...] = state * chunk_decay + _dot_tn(bm_b, xdd)
        y = y * z_ref[r, :]
        y = y * lax.rsqrt(jnp.mean(y * y, axis=-1, keepdims=True) + RMS_EPS) * ng_ref[...]
        y_ref[r, :] = y.astype(y_ref.dtype)


def _ssd(zs, xbc, dt_t, conv_w, conv_b, dt_bias, a_log, d_skip, norm_g, *, d_inner, rows):
    t = zs.shape[0]
    g = SSD_N_GROUPS
    gw = d_inner // g
    hg = gw // SSD_HEAD_DIM
    n = SSD_D_STATE
    assert t % rows == 0 and rows % SSD_CHUNK == 0 and d_inner % n == 0
    assert 2 * hg <= SSD_CHUNK and hg % V7X_SUBLANES == 0
    conv_b2 = conv_b.reshape(1, -1)
    per_head = lambda v: v.reshape(g * hg, 1)
    dskip_l = jnp.repeat(d_skip, SSD_HEAD_DIM).reshape(1, d_inner)
    ng = norm_g.reshape(1, d_inner)
    b_blk = d_inner // n
    c_blk = b_blk + g

    head_col = pl.BlockSpec((hg, 1), lambda gi, ci: (gi, 0))
    in_specs = [
        pl.BlockSpec((rows, gw), lambda gi, ci: (ci, gi)),
        pl.BlockSpec((rows, n), lambda gi, ci: (ci, b_blk + gi)),
        pl.BlockSpec((rows, n), lambda gi, ci: (ci, c_blk + gi)),
        pl.BlockSpec((rows, gw), lambda gi, ci: (ci, gi)),
        pl.BlockSpec((hg, rows), lambda gi, ci: (gi, ci)),
        pl.BlockSpec((SSD_CONV_WIDTH, gw), lambda gi, ci: (0, gi)),
        pl.BlockSpec((SSD_CONV_WIDTH, n), lambda gi, ci: (0, b_blk + gi)),
        pl.BlockSpec((SSD_CONV_WIDTH, n), lambda gi, ci: (0, c_blk + gi)),
        pl.BlockSpec((1, gw), lambda gi, ci: (0, gi)),
        pl.BlockSpec((1, n), lambda gi, ci: (0, b_blk + gi)),
        pl.BlockSpec((1, n), lambda gi, ci: (0, c_blk + gi)),
        head_col, head_col,
        pl.BlockSpec((1, gw), lambda gi, ci: (0, gi)),
        pl.BlockSpec((1, gw), lambda gi, ci: (0, gi)),
    ]
    pad_rows = rows + V7X_SUBLANES
    block_bytes = (2 * (2 * rows * gw * 4 + 2 * rows * n * 4 + rows * gw * 2) + pad_rows * (gw + 2 * n) * 4
                   + rows * gw * 4 + 2 * rows * n * 2 + n * gw * 4)
    return pl.pallas_call(
        _ssd_kernel,
        grid=(g, t // rows),
        in_specs=in_specs,
        out_specs=pl.BlockSpec((rows, gw), lambda gi, ci: (ci, gi)),
        out_shape=jax.ShapeDtypeStruct((t, d_inner), BF16),
        scratch_shapes=[pltpu.VMEM((pad_rows, gw), F32), pltpu.VMEM((pad_rows, n), F32),
                        pltpu.VMEM((pad_rows, n), F32), pltpu.VMEM((rows, gw), F32),
                        pltpu.VMEM((rows, n), BF16), pltpu.VMEM((rows, n), BF16),
                        pltpu.VMEM((n, gw), F32)],
        compiler_params=pltpu.CompilerParams(
            dimension_semantics=("parallel", "arbitrary"), vmem_limit_bytes=_vmem_limit(block_bytes)),
        name="ssd",
    )(xbc, xbc, xbc, zs, dt_t, conv_w, conv_w, conv_w, conv_b2, conv_b2, conv_b2,
      per_head(dt_bias), per_head(a_log), dskip_l, ng)


def _swa_kernel(q_ref, k_ref, v_ref, pos_ref, invf_ref, sink_ref, o_ref, kprev_ref, vprev_ref):
    w = ATTN_WINDOW
    dh = ATTN_HEAD_DIM
    half = dh // 2
    lanes = 2 * dh
    n_kv = k_ref.shape[1] // dh
    tiles_per_kv = ATTN_GROUP // 2
    nb = pl.program_id(0)

    @pl.when(nb == 0)
    def _():
        kprev_ref[...] = jnp.zeros_like(kprev_ref)
        vprev_ref[...] = jnp.zeros_like(vprev_ref)

    lane = lax.broadcasted_iota(jnp.int32, (w, lanes), 1)
    first_half = (lane % dh) < half
    lo2 = lax.broadcasted_iota(jnp.int32, (2 * w, lanes), 1) < dh

    def attend(rb, prev_ok):
        ang = pos_ref[rb, :].astype(F32) * invf_ref[...]
        cos = jnp.cos(ang)
        sin = jnp.sin(ang)
        sin = jnp.where(first_half, -sin, sin)
        scale = dh ** -0.5

        def rope(u, c, s):
            partner = jnp.where(first_half, pltpu.roll(u, lanes - half, 1), pltpu.roll(u, half, 1))
            return u * c + partner * s

        kj = lax.broadcasted_iota(jnp.int32, (2 * w, w), 0)
        qi = lax.broadcasted_iota(jnp.int32, (2 * w, w), 1)
        valid = (kj > qi) & (kj <= qi + w) & ((kj >= w) | prev_ok)
        bias = jnp.where(valid, 0.0, -jnp.inf)

        cos_q, sin_q = cos * (scale * LOG2_E), sin * (scale * LOG2_E)
        q_tiles = [rope(q_ref[rb, t * lanes:(t + 1) * lanes], cos_q, sin_q).astype(BF16)
                   for t in range(q_ref.shape[1] // lanes)]

        units = []
        for pt in range(n_kv // 2):
            sl = slice(pt * lanes, (pt + 1) * lanes)
            k_cur = rope(k_ref[rb, sl], cos, sin)
            v_cur = v_ref[rb, sl]
            k_pair = jnp.concatenate([kprev_ref[:, sl], k_cur], axis=0)
            v_pair = jnp.concatenate([vprev_ref[:, sl], v_cur], axis=0)
            kprev_ref[:, sl] = k_cur
            vprev_ref[:, sl] = v_cur
            k_swap = pltpu.roll(k_pair, dh, 1)
            vt_pair = v_pair.T
            for side in range(2):
                hk = 2 * pt + side
                k_lo = jnp.where(lo2, k_pair if side == 0 else k_swap, 0.0).astype(BF16)
                k_hi = jnp.where(lo2, 0.0, k_swap if side == 0 else k_pair).astype(BF16)
                vt = vt_pair[side * dh:(side + 1) * dh, :].astype(BF16)
                units += [(hk, 0, k_lo, vt), (hk, 1, k_hi, vt)]

        scores = []
        for hk, parity, k_sel, vt in units:
            qs = jnp.concatenate(q_tiles[hk * tiles_per_kv:(hk + 1) * tiles_per_kv], axis=0)
            scores.append(_dot_nt(k_sel, qs))

        probs = []
        for (hk, parity, k_sel, vt), st in zip(units, scores):
            p_parts, den_parts = [], []
            for j in range(tiles_per_kv):
                head = hk * ATTN_GROUP + 2 * j + parity
                s = st[:, j * w:(j + 1) * w] + bias
                sink = sink_ref[0:1, head:head + 1] * LOG2_E
                m = jnp.maximum(jnp.max(s, axis=0, keepdims=True), sink)
                p = jnp.exp2(s - m)
                den_parts.append(jnp.sum(p, axis=0, keepdims=True) + jnp.exp2(sink - m))
                p_parts.append(p.astype(BF16))
            probs.append((jnp.concatenate(p_parts, axis=1), jnp.concatenate(den_parts, axis=1)))

        outs = {}
        for (hk, parity, k_sel, vt), (pt_all, den) in zip(units, probs):
            outs[hk, parity] = _dot(vt, pt_all) / den

        for hk in range(n_kv):
            for j in range(tiles_per_kv):
                qt = hk * tiles_per_kv + j
                tile_t = jnp.concatenate([outs[hk, 0][:, j * w:(j + 1) * w],
                                          outs[hk, 1][:, j * w:(j + 1) * w]], axis=0)
                o_ref[rb, qt * lanes:(qt + 1) * lanes] = tile_t.T.astype(o_ref.dtype)

    for blk in range(q_ref.shape[0] // w):
        attend(slice(blk * w, (blk + 1) * w), (nb > 0) if blk == 0 else True)


def _swa(proj, positions, sinks, *, q_col, q_width, k_col, v_col, kv_width):
    t = proj.shape[0]
    w = ATTN_WINDOW
    assert t % w == 0 and q_col % q_width == 0 and k_col % kv_width == 0 and v_col % kv_width == 0
    half = ATTN_HEAD_DIM // 2
    inv_freq = ROPE_THETA ** (-jnp.arange(half, dtype=F32) * 2.0 / ATTN_HEAD_DIM)
    inv_freq = jnp.tile(inv_freq, 2 * V7X_LANES // ATTN_HEAD_DIM).reshape(1, V7X_LANES)
    rows = min(SWA_BLOCKS_PER_STEP * w, t)
    assert t % rows == 0
    block_bytes = 2 * (rows * q_width * 4 + 2 * rows * kv_width * 4 + rows * q_width * 2) + 2 * w * kv_width * 4
    return pl.pallas_call(
        _swa_kernel,
        grid=(t // rows,),
        in_specs=[
            pl.BlockSpec((rows, q_width), lambda i: (i, q_col // q_width)),
            pl.BlockSpec((rows, kv_width), lambda i: (i, k_col // kv_width)),
            pl.BlockSpec((rows, kv_width), lambda i: (i, v_col // kv_width)),
            pl.BlockSpec((rows, 1), lambda i: (i, 0)),
            pl.BlockSpec((1, V7X_LANES), lambda i: (0, 0)),
            pl.BlockSpec((1, sinks.shape[0]), lambda i: (0, 0)),
        ],
        out_specs=pl.BlockSpec((rows, q_width), lambda i: (i, 0)),
        out_shape=jax.ShapeDtypeStruct((t, q_width), BF16),
        scratch_shapes=[pltpu.VMEM((w, kv_width), F32), pltpu.VMEM((w, kv_width), F32)],
        compiler_params=pltpu.CompilerParams(
            dimension_semantics=("arbitrary",), vmem_limit_bytes=_vmem_limit(block_bytes)),
        name="swa",
    )(proj, proj, proj, positions.reshape(t, 1), inv_freq, sinks.reshape(1, -1))


def _merge_kernel(ys_ref, ya_ref, ws_ref, wa_ref, gs_ref, ga_ref, o_ref, wsb_ref, wab_ref):
    @pl.when(pl.program_id(1) == 0)
    def _():
        _cast_weight(wsb_ref, ws_ref)
        _cast_weight(wab_ref, wa_ref)

    tn = o_ref.shape[1]
    for c0 in range(0, tn, tn // 2):
        cols = slice(c0, c0 + tn // 2)
        y_s = _dot(ys_ref[...], wsb_ref[:, cols])
        y_a = _dot(ya_ref[...], wab_ref[:, cols])
        merged = jax.nn.sigmoid(gs_ref[:, cols]) * y_s + jax.nn.sigmoid(ga_ref[:, cols]) * y_a
        o_ref[:, cols] = merged.astype(o_ref.dtype)


def _merge(y_ssd, y_attn, w_ssd_o, w_attn_o, proj, *, gs_col, ga_col, tm, tn):
    t, ks = y_ssd.shape
    ka = y_attn.shape[1]
    d = w_ssd_o.shape[1]
    assert t % tm == 0 and d % tn == 0 and gs_col % tn == 0 and ga_col % tn == 0
    block_bytes = (2 * (tm * ks * 2 + tm * ka * 2 + ks * tn * 4 + ka * tn * 4 + 2 * tm * tn * 4 + tm * tn * 2)
                   + (ks + ka) * tn * 2)
    return pl.pallas_call(
        _merge_kernel,
        grid=(d // tn, t // tm),
        in_specs=[pl.BlockSpec((tm, ks), lambda j, i: (i, 0)),
                  pl.BlockSpec((tm, ka), lambda j, i: (i, 0)),
                  pl.BlockSpec((ks, tn), lambda j, i: (0, j)),
                  pl.BlockSpec((ka, tn), lambda j, i: (0, j)),
                  pl.BlockSpec((tm, tn), lambda j, i: (i, gs_col // tn + j)),
                  pl.BlockSpec((tm, tn), lambda j, i: (i, ga_col // tn + j))],
        out_specs=pl.BlockSpec((tm, tn), lambda j, i: (i, j)),
        out_shape=jax.ShapeDtypeStruct((t, d), BF16),
        scratch_shapes=[pltpu.VMEM((ks, tn), BF16), pltpu.VMEM((ka, tn), BF16)],
        compiler_params=pltpu.CompilerParams(
            dimension_semantics=("parallel", "arbitrary"), vmem_limit_bytes=_vmem_limit(block_bytes)),
        name="merge",
    )(y_ssd, y_attn, w_ssd_o, w_attn_o, proj, proj)


def _proj_ln_kernel(m_ref, w_ref, h_ref, g_ref, b_ref, o_ref, wb_ref):
    @pl.when(pl.program_id(0) == 0)
    def _():
        _cast_weight(wb_ref, w_ref)

    y = DEEPNORM_ALPHA * h_ref[...] + _dot(m_ref[...], wb_ref[...])
    o_ref[...] = _layer_norm(y, g_ref[...], b_ref[...])


def _proj_ln(merged, w_out, h, g, b, *, tm):
    t, d = h.shape
    k = merged.shape[1]
    assert t % tm == 0
    vec = pl.BlockSpec((1, d), lambda i: (0, 0))
    block_bytes = 2 * (tm * k * 2 + 2 * tm * d * 4) + k * d * 4 + k * d * 2
    return pl.pallas_call(
        _proj_ln_kernel,
        grid=(t // tm,),
        in_specs=[pl.BlockSpec((tm, k), lambda i: (i, 0)),
                  pl.BlockSpec((k, d), lambda i: (0, 0), pipeline_mode=pl.Buffered(1)),
                  pl.BlockSpec((tm, d), lambda i: (i, 0)),
                  vec, vec],
        out_specs=pl.BlockSpec((tm, d), lambda i: (i, 0)),
        out_shape=jax.ShapeDtypeStruct((t, d), F32),
        scratch_shapes=[pltpu.VMEM((k, d), BF16)],
        compiler_params=pltpu.CompilerParams(
            dimension_semantics=("arbitrary",), vmem_limit_bytes=_vmem_limit(block_bytes)),
        name="proj_ln",
    )(merged, w_out, h, g, b)


def _layer(x, positions, ffn1_w_gate, ffn1_w_up, ffn1_w_down, ln1_g, ln1_b,
           w_in, conv_w, conv_b, dt_bias, a_log, d_skip, ssd_norm_g, w_ssd_o,
           attn_sinks, w_attn_o, w_out, ln2_g, ln2_b,
           ffn2_w_gate, ffn2_w_up, ffn2_w_down, ln3_g, ln3_b):
    t, d = x.shape
    d_inner = w_ssd_o.shape[0]
    n_ssd_heads = dt_bias.shape[0]
    q_width = w_attn_o.shape[0]
    kv_width = q_width // ATTN_GROUP
    xbc_width = d_inner + 2 * SSD_N_GROUPS * SSD_D_STATE
    vec = lambda v: v.reshape(1, -1)
    tm_ffn = min(512, t)
    tm_mm = min(1024, t)

    sizes = (d_inner, xbc_width, n_ssd_heads, q_width, kv_width, kv_width, d, d)
    starts = [0]
    for s in sizes:
        starts.append(starts[-1] + s)
    dt_lo, dt_hi, total = starts[2], starts[3], starts[-1]

    h1, h1b, wg2, wu2, wd2 = _ffn_ln(
        x, ffn1_w_gate.astype(BF16), ffn1_w_up.astype(BF16), ffn1_w_down.astype(BF16), vec(ln1_g), vec(ln1_b),
        tm=tm_ffn, th=512, emit_bf16=True, name="ffn1_ln", side_cast=(ffn2_w_gate, ffn2_w_up, ffn2_w_down))

    w_in_t = w_in.T
    n_qkv = q_width + 2 * kv_width
    zs = _in_proj(h1b, w_in_t, row0=0, n=d_inner, tm=tm_mm, tn=1024, weight_buffers=2, name="in_proj_z",
                  apply_silu=True)
    xbc = _in_proj(h1b, w_in_t, row0=d_inner, n=xbc_width, tm=tm_mm, tn=1024, weight_buffers=2,
                   name="in_proj_xbc")
    qkv = _in_proj(h1b, w_in_t, row0=dt_hi, n=n_qkv, tm=tm_mm, tn=n_qkv // 2, weight_buffers=2,
                   name="in_proj_qkv")
    gates = _in_proj(h1b, w_in_t, row0=dt_hi + n_qkv, n=2 * d, tm=tm_mm, tn=1024, weight_buffers=2,
                     name="in_proj_gates")
    dt_t = _dt_proj(h1b, w_in_t, row0=dt_lo, n_heads=n_ssd_heads, tm=tm_mm)

    y_ssd = _ssd(zs, xbc, dt_t, conv_w, conv_b, dt_bias, a_log, d_skip, ssd_norm_g,
                 d_inner=d_inner, rows=min(SSD_ROWS_PER_STEP, t))
    y_attn = _swa(qkv, positions, attn_sinks, q_col=0, q_width=q_width,
                  k_col=q_width, v_col=q_width + kv_width, kv_width=kv_width)

    merged = _merge(y_ssd, y_attn, w_ssd_o, w_attn_o, gates, gs_col=0, ga_col=d, tm=tm_ffn, tn=512)
    h2 = _proj_ln(merged, w_out, h1, vec(ln2_g), vec(ln2_b), tm=tm_ffn)

    (out,) = _ffn_ln(h2, wg2, wu2, wd2, vec(ln3_g), vec(ln3_b), tm=tm_ffn, th=512, emit_bf16=False, name="ffn2_ln")
    return out


def kernel(x, positions, ffn1_w_gate, ffn1_w_up, ffn1_w_down, ln1_g, ln1_b, w_in, conv_w, conv_b, dt_bias, a_log, d_skip, ssd_norm_g, w_ssd_o, attn_sinks, w_attn_o, w_out, ln2_g, ln2_b, ffn2_w_gate, ffn2_w_up, ffn2_w_down, ln3_g, ln3_b):
    batch, depth = x.shape[0], ffn1_w_gate.shape[0]
    assert depth == DEPTH
    outs = []
    for bi in range(batch):
        h = x[bi]
        for l in range(depth):
            h = _layer(h, positions[bi], ffn1_w_gate[l], ffn1_w_up[l], ffn1_w_down[l], ln1_g[l], ln1_b[l],
                       w_in[l], conv_w[l], conv_b[l], dt_bias[l], a_log[l], d_skip[l], ssd_norm_g[l], w_ssd_o[l],
                       attn_sinks[l], w_attn_o[l], w_out[l], ln2_g[l], ln2_b[l],
                       ffn2_w_gate[l], ffn2_w_up[l], ffn2_w_down[l], ln3_g[l], ln3_b[l])
        outs.append(h)
    return jnp.stack(outs, axis=0)
```

```python
import functools
import math

import jax
import jax.numpy as jnp
from jax import lax
from jax.experimental import pallas as pl
from jax.experimental.pallas import tpu as pltpu

F32 = jnp.float32
BF16 = jnp.bfloat16

V7X_LANES = 128
V7X_SUBLANES = 8
V7X_VMEM_BYTES = 64 * 1024 * 1024

SSD_HEAD_DIM = 64
SSD_N_GROUPS = 8
SSD_D_STATE = 128
SSD_CONV_WIDTH = 4
SSD_CHUNK = 128
ATTN_HEAD_DIM = 64
ATTN_GROUP = 8
ATTN_WINDOW = 128
ROPE_THETA = 10000.0
DEPTH = 1
DEEPNORM_ALPHA = (2 * DEPTH) ** 0.25
LN_EPS = 1e-5
RMS_EPS = 1e-5
LOG2_E = math.log2(math.e)

CAST_ROWS = 256
SSD_ROWS_PER_STEP = 16 * SSD_CHUNK
FFN_HIDDEN_SLICES = 2
SWA_BLOCKS_PER_STEP = 1
IN_PROJ_RING_SLOTS = 3
FFN_RING_SLOTS = 3


def _vmem_limit(block_bytes):
    return int(min(block_bytes * 5 // 4 + (8 << 20), V7X_VMEM_BYTES - (6 << 20)))


def _dot(a, b):
    return jnp.dot(a, b, preferred_element_type=F32)


def _dot_nt(a, b):
    return lax.dot_general(a, b, (((1,), (1,)), ((), ())), preferred_element_type=F32)


def _dot_tn(a, b):
    return lax.dot_general(a, b, (((0,), (0,)), ((), ())), preferred_element_type=F32)


def _dot_f32(a, b):
    return jnp.dot(a, b, preferred_element_type=F32, precision=lax.Precision.HIGHEST)


def _layer_norm(y, g, b):
    mu = jnp.mean(y, axis=-1, keepdims=True)
    yc = y - mu
    var = jnp.mean(yc * yc, axis=-1, keepdims=True)
    return yc * lax.rsqrt(var + LN_EPS) * g + b


def _silu(x):
    return x * jax.nn.sigmoid(x)


def _cast_weight(dst_ref, w_ref):
    k = dst_ref.shape[0]
    for r0 in range(0, k, CAST_ROWS):
        rows = slice(r0, min(r0 + CAST_ROWS, k))
        dst_ref[rows, :] = w_ref[rows, :].astype(BF16)


def _ffn_ln_kernel(x_ref, wg_hbm, wu_hbm, wd_hbm, g_ref, b_ref, *refs, n_hidden_tiles, emit_bf16, n_side):
    side_in, refs = refs[:n_side], refs[n_side:]
    if emit_bf16:
        o_ref, ob_ref = refs[:2]
        refs = refs[2:]
    else:
        o_ref = refs[0]
        refs = refs[1:]
    side_out, (xb_ref, acc_ref, wg_buf, wu_buf, wd_buf, sem_ref) = refs[:n_side], refs[n_side:]
    j = pl.program_id(1)

    th = wg_buf.shape[2]
    step = pl.program_id(0) * n_hidden_tiles + j
    total = pl.num_programs(0) * n_hidden_tiles

    def tile_copies(s):
        col = pl.multiple_of(lax.rem(s, n_hidden_tiles) * th, th)
        slot = lax.rem(s, FFN_RING_SLOTS)
        return (pltpu.make_async_copy(wg_hbm.at[:, pl.ds(col, th)], wg_buf.at[slot], sem_ref.at[0, slot]),
                pltpu.make_async_copy(wu_hbm.at[:, pl.ds(col, th)], wu_buf.at[slot], sem_ref.at[1, slot]),
                pltpu.make_async_copy(wd_hbm.at[pl.ds(col, th), :], wd_buf.at[slot], sem_ref.at[2, slot]))

    @pl.when(step == 0)
    def _():
        for s in range(FFN_RING_SLOTS - 1):
            for cp in tile_copies(s):
                cp.start()

    @pl.when(step + FFN_RING_SLOTS - 1 < total)
    def _():
        for cp in tile_copies(step + FFN_RING_SLOTS - 1):
            cp.start()

    for src_ref, dst_ref in zip(side_in, side_out):
        dst_ref[...] = src_ref[...].astype(BF16)

    @pl.when(j == 0)
    def _():
        xb_ref[...] = x_ref[...].astype(BF16)
        acc_ref[...] = jnp.zeros_like(acc_ref)

    for cp in tile_copies(step):
        cp.wait()
    slot = lax.rem(step, FFN_RING_SLOTS)
    wg_ref, wu_ref, wd_ref = wg_buf.at[slot], wu_buf.at[slot], wd_buf.at[slot]

    xb = xb_ref[...]
    partial = None
    for h0 in range(0, th, th // FFN_HIDDEN_SLICES):
        cols = slice(h0, h0 + th // FFN_HIDDEN_SLICES)
        gate = _dot(xb, wg_ref[:, cols])
        up = _dot(xb, wu_ref[:, cols])
        act = (_silu(gate) * up).astype(BF16)
        down = _dot(act, wd_ref[cols, :])
        partial = down if partial is None else partial + down
    acc_ref[...] += partial

    @pl.when(j == n_hidden_tiles - 1)
    def _():
        y = DEEPNORM_ALPHA * x_ref[...] + 0.5 * acc_ref[...]
        out = _layer_norm(y, g_ref[...], b_ref[...])
        o_ref[...] = out
        if emit_bf16:
            ob_ref[...] = out.astype(BF16)


def _ffn_ln(x, wg, wu, wd, g, b, *, tm, th, emit_bf16, name, side_cast=()):
    t, d = x.shape
    hidden = wg.shape[1]
    assert t % tm == 0 and hidden % th == 0
    n_i, n_j = t // tm, hidden // th
    row = pl.BlockSpec((tm, d), lambda i, j: (i, 0))
    vec = pl.BlockSpec((1, d), lambda i, j: (0, 0))
    out_shape = [jax.ShapeDtypeStruct((t, d), F32)]
    out_specs = [row]
    if emit_bf16:
        out_shape.append(jax.ShapeDtypeStruct((t, d), BF16))
        out_specs.append(row)
    assert n_i * n_j >= FFN_RING_SLOTS - 1
    block_bytes = (2 * tm * d * 4 + FFN_RING_SLOTS * 3 * d * th * 2 + 2 * tm * d * 4
                   + (2 * tm * d * 2 if emit_bf16 else 0)
                   + tm * d * 2 + tm * d * 4)
    side_specs = []
    for w in side_cast:
        r, c = w.shape
        if r % n_i == 0 and c % n_j == 0 and (c // n_j) % V7X_LANES == 0:
            spec = pl.BlockSpec((r // n_i, c // n_j), lambda i, j: (i, j))
        else:
            assert r % n_j == 0 and c % n_i == 0 and (c // n_i) % V7X_LANES == 0
            spec = pl.BlockSpec((r // n_j, c // n_i), lambda i, j: (j, i))
        side_specs.append(spec)
        out_shape.append(jax.ShapeDtypeStruct((r, c), BF16))
        block_bytes += 2 * (r * c // (n_i * n_j)) * (4 + 2)
    return pl.pallas_call(
        functools.partial(_ffn_ln_kernel, n_hidden_tiles=n_j, emit_bf16=emit_bf16, n_side=len(side_cast)),
        grid=(n_i, n_j),
        in_specs=[row,
                  pl.BlockSpec(memory_space=pl.ANY),
                  pl.BlockSpec(memory_space=pl.ANY),
                  pl.BlockSpec(memory_space=pl.ANY),
                  vec, vec] + side_specs,
        out_specs=out_specs + side_specs,
        out_shape=out_shape,
        scratch_shapes=[pltpu.VMEM((tm, d), BF16), pltpu.VMEM((tm, d), F32),
                        pltpu.VMEM((FFN_RING_SLOTS, d, th), BF16), pltpu.VMEM((FFN_RING_SLOTS, d, th), BF16),
                        pltpu.VMEM((FFN_RING_SLOTS, th, d), BF16),
                        pltpu.SemaphoreType.DMA((3, FFN_RING_SLOTS))],
        compiler_params=pltpu.CompilerParams(
            dimension_semantics=("arbitrary", "arbitrary"), vmem_limit_bytes=_vmem_limit(block_bytes)),
        name=name,
    )(x, wg, wu, wd, g, b, *side_cast)


def _in_proj_kernel(a_hbm, w_ref, o_ref, wb_ref, abuf_ref, sem_ref, *, apply_silu):
    tm = abuf_ref.shape[1]
    n_i = pl.num_programs(1)
    total = pl.num_programs(0) * n_i
    step = pl.program_id(0) * n_i + pl.program_id(1)

    def row_tile_copy(s):
        row = pl.multiple_of(lax.rem(s, n_i) * tm, tm)
        slot = lax.rem(s, IN_PROJ_RING_SLOTS)
        return pltpu.make_async_copy(a_hbm.at[pl.ds(row, tm), :], abuf_ref.at[slot], sem_ref.at[slot])

    @pl.when(step == 0)
    def _():
        for s in range(IN_PROJ_RING_SLOTS - 1):
            row_tile_copy(s).start()

    @pl.when(step + IN_PROJ_RING_SLOTS - 1 < total)
    def _():
        row_tile_copy(step + IN_PROJ_RING_SLOTS - 1).start()

    @pl.when(pl.program_id(1) == 0)
    def _():
        _cast_weight(wb_ref, w_ref)

    row_tile_copy(step).wait()
    raw = _dot_nt(abuf_ref[lax.rem(step, IN_PROJ_RING_SLOTS)], wb_ref[...])
    o_ref[...] = _silu(raw) if apply_silu else raw


def _in_proj(a, w_t, *, row0, n, tm, tn, weight_buffers, name, apply_silu=False):
    m, k = a.shape
    assert m % tm == 0 and n % tn == 0 and row0 % V7X_SUBLANES == 0
    w_spec = pl.BlockSpec((pl.Element(tn), pl.Element(k)),
                          lambda j, i: (pl.multiple_of(row0 + j * tn, V7X_SUBLANES), 0),
                          pipeline_mode=pl.Buffered(weight_buffers))
    assert (n // tn) * (m // tm) >= IN_PROJ_RING_SLOTS - 1
    block_bytes = (IN_PROJ_RING_SLOTS * tm * k * 2 + 2 * tm * tn * 4 + weight_buffers * tn * k * 4 + tn * k * 2)
    return pl.pallas_call(
        functools.partial(_in_proj_kernel, apply_silu=apply_silu),
        grid=(n // tn, m // tm),
        in_specs=[pl.BlockSpec(memory_space=pl.ANY), w_spec],
        out_specs=pl.BlockSpec((tm, tn), lambda j, i: (i, j)),
        out_shape=jax.ShapeDtypeStruct((m, n), F32),
        scratch_shapes=[pltpu.VMEM((tn, k), BF16), pltpu.VMEM((IN_PROJ_RING_SLOTS, tm, k), BF16),
                        pltpu.SemaphoreType.DMA((IN_PROJ_RING_SLOTS,))],
        compiler_params=pltpu.CompilerParams(
            dimension_semantics=("arbitrary", "arbitrary"), vmem_limit_bytes=_vmem_limit(block_bytes)),
        name=name,
    )(a, w_t)


def _dt_proj_kernel(a_ref, w_ref, o_ref):
    o_ref[...] = _dot_nt(w_ref[...].astype(BF16), a_ref[...])


def _dt_proj(a, w_t, *, row0, n_heads, tm):
    m, k = a.shape
    assert row0 % n_heads == 0 and n_heads % V7X_SUBLANES == 0 and m % tm == 0
    block_bytes = 2 * (tm * k * 2 + n_heads * k * 4 + n_heads * tm * 4)
    return pl.pallas_call(
        _dt_proj_kernel,
        grid=(m // tm,),
        in_specs=[pl.BlockSpec((tm, k), lambda i: (i, 0)),
                  pl.BlockSpec((n_heads, k), lambda i: (row0 // n_heads, 0))],
        out_specs=pl.BlockSpec((n_heads, tm), lambda i: (0, i)),
        out_shape=jax.ShapeDtypeStruct((n_heads, m), F32),
        compiler_params=pltpu.CompilerParams(
            dimension_semantics=("parallel",), vmem_limit_bytes=_vmem_limit(block_bytes)),
        name="dt_proj",
    )(a, w_t)


def _ssd_kernel(x_ref, b_ref, c_ref, z_ref, dt_ref,
                cwx_ref, cwb_ref, cwc_ref, cbx_ref, cbb_ref, cbc_ref,
                bias_ref, alog_ref, dskip_ref, ng_ref,
                y_ref, xpad_ref, bpad_ref, cpad_ref, xs_ref, bm_ref, cm_ref, state_ref):
    q = SSD_CHUNK
    tail = V7X_SUBLANES
    rows = x_ref.shape[0]
    gw = x_ref.shape[1]
    heads = gw // SSD_HEAD_DIM
    pairs = heads // 2
    pw = 2 * SSD_HEAD_DIM

    @pl.when(pl.program_id(1) == 0)
    def _():
        xpad_ref[0:tail, :] = jnp.zeros((tail, xpad_ref.shape[1]), F32)
        bpad_ref[0:tail, :] = jnp.zeros((tail, bpad_ref.shape[1]), F32)
        cpad_ref[0:tail, :] = jnp.zeros((tail, cpad_ref.shape[1]), F32)
        state_ref[...] = jnp.zeros_like(state_ref)

    def conv_silu(u_ref, pad_ref, w_ref, cbias_ref, dst_ref):
        pad_ref[tail:tail + rows, :] = u_ref[...]
        acc = cbias_ref[...]
        for k in range(SSD_CONV_WIDTH):
            start = tail - (SSD_CONV_WIDTH - 1) + k
            acc = acc + w_ref[k:k + 1, :] * pad_ref[start:start + rows, :]
        pad_ref[0:tail, :] = pad_ref[rows:rows + tail, :]
        dst_ref[...] = _silu(acc).astype(dst_ref.dtype)

    conv_silu(x_ref, xpad_ref, cwx_ref, cbx_ref, xs_ref)
    conv_silu(b_ref, bpad_ref, cwb_ref, cbb_ref, bm_ref)
    conv_silu(c_ref, cpad_ref, cwc_ref, cbc_ref, cm_ref)

    dt_all = jax.nn.softplus(dt_ref[...] + bias_ref[...])
    adt_all = dt_all * (-jnp.exp(alog_ref[...]) * LOG2_E)

    ri = lax.broadcasted_iota(jnp.int32, (q, q), 0)
    ci = lax.broadcasted_iota(jnp.int32, (q, q), 1)
    causal = ci <= ri
    upper = (ri <= ci).astype(F32)
    lo = lax.broadcasted_iota(jnp.int32, (q, pw), 1) < SSD_HEAD_DIM
    lo_row = lax.broadcasted_iota(jnp.int32, (1, pw), 1) < SSD_HEAD_DIM

    staged = []
    for ck in range(rows // q):
        r = slice(ck * q, (ck + 1) * q)
        dt_r = dt_all[:, r]
        cs_r = _dot_f32(adt_all[:, r], upper)

        def over_lanes(row0, row1):
            stacked = jnp.concatenate([jnp.broadcast_to(row0, (SSD_HEAD_DIM, q)),
                                       jnp.broadcast_to(row1, (SSD_HEAD_DIM, q))], axis=0)
            return stacked.T

        cs_l = [over_lanes(cs_r[h:h + 1, :], cs_r[h:h + 1, :]) for h in range(heads)]

        bm_b = bm_ref[r, :]
        cm_b = cm_ref[r, :]
        cb = _dot_nt(cm_b, bm_b)

        y_parts, din_parts, xdd_parts, cd_parts = [], [], [], []
        for p in range(pairs):
            h0, h1 = 2 * p, 2 * p + 1
            sl = slice(p * pw, (p + 1) * pw)
            xp = xs_ref[r, sl]
            xd = xp * over_lanes(dt_r[h0:h0 + 1, :], dt_r[h1:h1 + 1, :])
            xd_b = xd.astype(BF16)
            y_diag = []
            for h in (h0, h1):
                seg = jnp.where(causal, jnp.exp2(cs_l[h] - cs_r[h:h + 1, :]), 0.0)
                y_diag.append(_dot((cb * seg).astype(BF16), xd_b))
            cs_p = jnp.where(lo, cs_l[h0], cs_l[h1])
            cs_end = jnp.where(lo_row, cs_l[h0][q - 1:q, :], cs_l[h1][q - 1:q, :])
            y_parts.append(jnp.where(lo, y_diag[0], y_diag[1]) + dskip_ref[:, sl] * xp)
            din_parts.append(jnp.exp2(cs_p))
            xdd_parts.append((xd * jnp.exp2(cs_end - cs_p)).astype(BF16))
            cd_parts.append(jnp.exp2(cs_end))
        staged.append((cm_b, bm_b, jnp.concatenate(y_parts, axis=1), jnp.concatenate(din_parts, axis=1),
                       jnp.concatenate(xdd_parts, axis=1), jnp.concatenate(cd_parts, axis=1)))

    for ck, (cm_b, bm_b, y_local, decay_in, xdd, chunk_decay) in enumerate(staged):
        r = slice(ck * q, (ck + 1) * q)
        state = state_ref[...]
        y = y_local + _dot(cm_b, state.astype(BF16)) * decay_in
        state_ref[...] = state * chunk_decay + _dot_tn(bm_b, xdd)
        y = y * z_ref[r, :]
        y = y * lax.rsqrt(jnp.mean(y * y, axis=-1, keepdims=True) + RMS_EPS) * ng_ref[...]
        y_ref[r, :] = y.astype(y_ref.dtype)


def _ssd(zs, xbc, dt_t, conv_w, conv_b, dt_bias, a_log, d_skip, norm_g, *, d_inner, rows):
    t = zs.shape[0]
    g = SSD_N_GROUPS
    gw = d_inner // g
    hg = gw // SSD_HEAD_DIM
    n = SSD_D_STATE
    assert t % rows == 0 and rows % SSD_CHUNK == 0 and d_inner % n == 0
    assert 2 * hg <= SSD_CHUNK and hg % V7X_SUBLANES == 0
    conv_b2 = conv_b.reshape(1, -1)
    per_head = lambda v: v.reshape(g * hg, 1)
    dskip_l = jnp.repeat(d_skip, SSD_HEAD_DIM).reshape(1, d_inner)
    ng = norm_g.reshape(1, d_inner)
    b_blk = d_inner // n
    c_blk = b_blk + g

    head_col = pl.BlockSpec((hg, 1), lambda gi, ci: (gi, 0))
    in_specs = [
        pl.BlockSpec((rows, gw), lambda gi, ci: (ci, gi)),
        pl.BlockSpec((rows, n), lambda gi, ci: (ci, b_blk + gi)),
        pl.BlockSpec((rows, n), lambda gi, ci: (ci, c_blk + gi)),
        pl.BlockSpec((rows, gw), lambda gi, ci: (ci, gi)),
        pl.BlockSpec((hg, rows), lambda gi, ci: (gi, ci)),
        pl.BlockSpec((SSD_CONV_WIDTH, gw), lambda gi, ci: (0, gi)),
        pl.BlockSpec((SSD_CONV_WIDTH, n), lambda gi, ci: (0, b_blk + gi)),
        pl.BlockSpec((SSD_CONV_WIDTH, n), lambda gi, ci: (0, c_blk + gi)),
        pl.BlockSpec((1, gw), lambda gi, ci: (0, gi)),
        pl.BlockSpec((1, n), lambda gi, ci: (0, b_blk + gi)),
        pl.BlockSpec((1, n), lambda gi, ci: (0, c_blk + gi)),
        head_col, head_col,
        pl.BlockSpec((1, gw), lambda gi, ci: (0, gi)),
        pl.BlockSpec((1, gw), lambda gi, ci: (0, gi)),
    ]
    pad_rows = rows + V7X_SUBLANES
    block_bytes = (2 * (2 * rows * gw * 4 + 2 * rows * n * 4 + rows * gw * 2) + pad_rows * (gw + 2 * n) * 4
                   + rows * gw * 4 + 2 * rows * n * 2 + n * gw * 4)
    return pl.pallas_call(
        _ssd_kernel,
        grid=(g, t // rows),
        in_specs=in_specs,
        out_specs=pl.BlockSpec((rows, gw), lambda gi, ci: (ci, gi)),
        out_shape=jax.ShapeDtypeStruct((t, d_inner), BF16),
        scratch_shapes=[pltpu.VMEM((pad_rows, gw), F32), pltpu.VMEM((pad_rows, n), F32),
                        pltpu.VMEM((pad_rows, n), F32), pltpu.VMEM((rows, gw), F32),
                        pltpu.VMEM((rows, n), BF16), pltpu.VMEM((rows, n), BF16),
                        pltpu.VMEM((n, gw), F32)],
        compiler_params=pltpu.CompilerParams(
            dimension_semantics=("parallel", "arbitrary"), vmem_limit_bytes=_vmem_limit(block_bytes)),
        name="ssd",
    )(xbc, xbc, xbc, zs, dt_t, conv_w, conv_w, conv_w, conv_b2, conv_b2, conv_b2,
      per_head(dt_bias), per_head(a_log), dskip_l, ng)


def _swa_kernel(q_ref, k_ref, v_ref, pos_ref, invf_ref, sink_ref, o_ref, kprev_ref, vprev_ref):
    w = ATTN_WINDOW
    dh = ATTN_HEAD_DIM
    half = dh // 2
    lanes = 2 * dh
    n_kv = k_ref.shape[1] // dh
    tiles_per_kv = ATTN_GROUP // 2
    nb = pl.program_id(0)

    @pl.when(nb == 0)
    def _():
        kprev_ref[...] = jnp.zeros_like(kprev_ref)
        vprev_ref[...] = jnp.zeros_like(vprev_ref)

    lane = lax.broadcasted_iota(jnp.int32, (w, lanes), 1)
    first_half = (lane % dh) < half
    lo2 = lax.broadcasted_iota(jnp.int32, (2 * w, lanes), 1) < dh

    def attend(rb, prev_ok):
        ang = pos_ref[rb, :].astype(F32) * invf_ref[...]
        cos = jnp.cos(ang)
        sin = jnp.sin(ang)
        sin = jnp.where(first_half, -sin, sin)
        scale = dh ** -0.5

        def rope(u, c, s):
            partner = jnp.where(first_half, pltpu.roll(u, lanes - half, 1), pltpu.roll(u, half, 1))
            return u * c + partner * s

        kj = lax.broadcasted_iota(jnp.int32, (2 * w, w), 0)
        qi = lax.broadcasted_iota(jnp.int32, (2 * w, w), 1)
        valid = (kj > qi) & (kj <= qi + w) & ((kj >= w) | prev_ok)
        bias = jnp.where(valid, 0.0, -jnp.inf)

        cos_q, sin_q = cos * (scale * LOG2_E), sin * (scale * LOG2_E)
        q_tiles = [rope(q_ref[rb, t * lanes:(t + 1) * lanes], cos_q, sin_q).astype(BF16)
                   for t in range(q_ref.shape[1] // lanes)]

        units = []
        for pt in range(n_kv // 2):
            sl = slice(pt * lanes, (pt + 1) * lanes)
            k_cur = rope(k_ref[rb, sl], cos, sin)
            v_cur = v_ref[rb, sl]
            k_pair = jnp.concatenate([kprev_ref[:, sl], k_cur], axis=0)
            v_pair = jnp.concatenate([vprev_ref[:, sl], v_cur], axis=0)
            kprev_ref[:, sl] = k_cur
            vprev_ref[:, sl] = v_cur
            k_swap = pltpu.roll(k_pair, dh, 1)
            vt_pair = v_pair.T
            for side in range(2):
                hk = 2 * pt + side
                k_lo = jnp.where(lo2, k_pair if side == 0 else k_swap, 0.0).astype(BF16)
                k_hi = jnp.where(lo2, 0.0, k_swap if side == 0 else k_pair).astype(BF16)
                vt = vt_pair[side * dh:(side + 1) * dh, :].astype(BF16)
                units += [(hk, 0, k_lo, vt), (hk, 1, k_hi, vt)]

        scores = []
        for hk, parity, k_sel, vt in units:
            qs = jnp.concatenate(q_tiles[hk * tiles_per_kv:(hk + 1) * tiles_per_kv], axis=0)
            scores.append(_dot_nt(k_sel, qs))

        probs = []
        for (hk, parity, k_sel, vt), st in zip(units, scores):
            p_parts, den_parts = [], []
            for j in range(tiles_per_kv):
                head = hk * ATTN_GROUP + 2 * j + parity
                s = st[:, j * w:(j + 1) * w] + bias
                sink = sink_ref[0:1, head:head + 1] * LOG2_E
                m = jnp.maximum(jnp.max(s, axis=0, keepdims=True), sink)
                p = jnp.exp2(s - m)
                den_parts.append(jnp.sum(p, axis=0, keepdims=True) + jnp.exp2(sink - m))
                p_parts.append(p.astype(BF16))
            probs.append((jnp.concatenate(p_parts, axis=1), jnp.concatenate(den_parts, axis=1)))

        outs = {}
        for (hk, parity, k_sel, vt), (pt_all, den) in zip(units, probs):
            outs[hk, parity] = _dot(vt, pt_all) / den

        for hk in range(n_kv):
            for j in range(tiles_per_kv):
                qt = hk * tiles_per_kv + j
                tile_t = jnp.concatenate([outs[hk, 0][:, j * w:(j + 1) * w],
                                          outs[hk, 1][:, j * w:(j + 1) * w]], axis=0)
                o_ref[rb, qt * lanes:(qt + 1) * lanes] = tile_t.T.astype(o_ref.dtype)

    for blk in range(q_ref.shape[0] // w):
        attend(slice(blk * w, (blk + 1) * w), (nb > 0) if blk == 0 else True)


def _swa(proj, positions, sinks, *, q_col, q_width, k_col, v_col, kv_width):
    t = proj.shape[0]
    w = ATTN_WINDOW
    assert t % w == 0 and q_col % q_width == 0 and k_col % kv_width == 0 and v_col % kv_width == 0
    half = ATTN_HEAD_DIM // 2
    inv_freq = ROPE_THETA ** (-jnp.arange(half, dtype=F32) * 2.0 / ATTN_HEAD_DIM)
    inv_freq = jnp.tile(inv_freq, 2 * V7X_LANES // ATTN_HEAD_DIM).reshape(1, V7X_LANES)
    rows = min(SWA_BLOCKS_PER_STEP * w, t)
    assert t % rows == 0
    block_bytes = 2 * (rows * q_width * 4 + 2 * rows * kv_width * 4 + rows * q_width * 2) + 2 * w * kv_width * 4
    return pl.pallas_call(
        _swa_kernel,
        grid=(t // rows,),
        in_specs=[
            pl.BlockSpec((rows, q_width), lambda i: (i, q_col // q_width)),
            pl.BlockSpec((rows, kv_width), lambda i: (i, k_col // kv_width)),
            pl.BlockSpec((rows, kv_width), lambda i: (i, v_col // kv_width)),
            pl.BlockSpec((rows, 1), lambda i: (i, 0)),
            pl.BlockSpec((1, V7X_LANES), lambda i: (0, 0)),
            pl.BlockSpec((1, sinks.shape[0]), lambda i: (0, 0)),
        ],
        out_specs=pl.BlockSpec((rows, q_width), lambda i: (i, 0)),
        out_shape=jax.ShapeDtypeStruct((t, q_width), BF16),
        scratch_shapes=[pltpu.VMEM((w, kv_width), F32), pltpu.VMEM((w, kv_width), F32)],
        compiler_params=pltpu.CompilerParams(
            dimension_semantics=("arbitrary",), vmem_limit_bytes=_vmem_limit(block_bytes)),
        name="swa",
    )(proj, proj, proj, positions.reshape(t, 1), inv_freq, sinks.reshape(1, -1))


def _merge_kernel(ys_ref, ya_ref, ws_ref, wa_ref, gs_ref, ga_ref, o_ref, wsb_ref, wab_ref):
    @pl.when(pl.program_id(1) == 0)
    def _():
        _cast_weight(wsb_ref, ws_ref)
        _cast_weight(wab_ref, wa_ref)

    tn = o_ref.shape[1]
    for c0 in range(0, tn, tn // 2):
        cols = slice(c0, c0 + tn // 2)
        y_s = _dot(ys_ref[...], wsb_ref[:, cols])
        y_a = _dot(ya_ref[...], wab_ref[:, cols])
        merged = jax.nn.sigmoid(gs_ref[:, cols]) * y_s + jax.nn.sigmoid(ga_ref[:, cols]) * y_a
        o_ref[:, cols] = merged.astype(o_ref.dtype)


def _merge(y_ssd, y_attn, w_ssd_o, w_attn_o, proj, *, gs_col, ga_col, tm, tn):
    t, ks = y_ssd.shape
    ka = y_attn.shape[1]
    d = w_ssd_o.shape[1]
    assert t % tm == 0 and d % tn == 0 and gs_col % tn == 0 and ga_col % tn == 0
    block_bytes = (2 * (tm * ks * 2 + tm * ka * 2 + ks * tn * 4 + ka * tn * 4 + 2 * tm * tn * 4 + tm * tn * 2)
                   + (ks + ka) * tn * 2)
    return pl.pallas_call(
        _merge_kernel,
        grid=(d // tn, t // tm),
        in_specs=[pl.BlockSpec((tm, ks), lambda j, i: (i, 0)),
                  pl.BlockSpec((tm, ka), lambda j, i: (i, 0)),
                  pl.BlockSpec((ks, tn), lambda j, i: (0, j)),
                  pl.BlockSpec((ka, tn), lambda j, i: (0, j)),
                  pl.BlockSpec((tm, tn), lambda j, i: (i, gs_col // tn + j)),
                  pl.BlockSpec((tm, tn), lambda j, i: (i, ga_col // tn + j))],
        out_specs=pl.BlockSpec((tm, tn), lambda j, i: (i, j)),
        out_shape=jax.ShapeDtypeStruct((t, d), BF16),
        scratch_shapes=[pltpu.VMEM((ks, tn), BF16), pltpu.VMEM((ka, tn), BF16)],
        compiler_params=pltpu.CompilerParams(
            dimension_semantics=("parallel", "arbitrary"), vmem_limit_bytes=_vmem_limit(block_bytes)),
        name="merge",
    )(y_ssd, y_attn, w_ssd_o, w_attn_o, proj, proj)


def _proj_ln_kernel(m_ref, w_ref, h_ref, g_ref, b_ref, o_ref, wb_ref):
    @pl.when(pl.program_id(0) == 0)
    def _():
        _cast_weight(wb_ref, w_ref)

    y = DEEPNORM_ALPHA * h_ref[...] + _dot(m_ref[...], wb_ref[...])
    o_ref[...] = _layer_norm(y, g_ref[...], b_ref[...])


def _proj_ln(merged, w_out, h, g, b, *, tm):
    t, d = h.shape
    k = merged.shape[1]
    assert t % tm == 0
    vec = pl.BlockSpec((1, d), lambda i: (0, 0))
    block_bytes = 2 * (tm * k * 2 + 2 * tm * d * 4) + k * d * 4 + k * d * 2
    return pl.pallas_call(
        _proj_ln_kernel,
        grid=(t // tm,),
        in_specs=[pl.BlockSpec((tm, k), lambda i: (i, 0)),
                  pl.BlockSpec((k, d), lambda i: (0, 0), pipeline_mode=pl.Buffered(1)),
                  pl.BlockSpec((tm, d), lambda i: (i, 0)),
                  vec, vec],
        out_specs=pl.BlockSpec((tm, d), lambda i: (i, 0)),
        out_shape=jax.ShapeDtypeStruct((t, d), F32),
        scratch_shapes=[pltpu.VMEM((k, d), BF16)],
        compiler_params=pltpu.CompilerParams(
            dimension_semantics=("arbitrary",), vmem_limit_bytes=_vmem_limit(block_bytes)),
        name="proj_ln",
    )(merged, w_out, h, g, b)


def _layer(x, positions, ffn1_w_gate, ffn1_w_up, ffn1_w_down, ln1_g, ln1_b,
           w_in, conv_w, conv_b, dt_bias, a_log, d_skip, ssd_norm_g, w_ssd_o,
           attn_sinks, w_attn_o, w_out, ln2_g, ln2_b,
           ffn2_w_gate, ffn2_w_up, ffn2_w_down, ln3_g, ln3_b):
    t, d = x.shape
    d_inner = w_ssd_o.shape[0]
    n_ssd_heads = dt_bias.shape[0]
    q_width = w_attn_o.shape[0]
    kv_width = q_width // ATTN_GROUP
    xbc_width = d_inner + 2 * SSD_N_GROUPS * SSD_D_STATE
    vec = lambda v: v.reshape(1, -1)
    tm_ffn = min(512, t)
    tm_mm = min(1024, t)

    sizes = (d_inner, xbc_width, n_ssd_heads, q_width, kv_width, kv_width, d, d)
    starts = [0]
    for s in sizes:
        starts.append(starts[-1] + s)
    dt_lo, dt_hi, total = starts[2], starts[3], starts[-1]

    h1, h1b, wg2, wu2, wd2 = _ffn_ln(
        x, ffn1_w_gate.astype(BF16), ffn1_w_up.astype(BF16), ffn1_w_down.astype(BF16), vec(ln1_g), vec(ln1_b),
        tm=tm_ffn, th=512, emit_bf16=True, name="ffn1_ln", side_cast=(ffn2_w_gate, ffn2_w_up, ffn2_w_down))

    w_in_t = w_in.T
    n_qkv = q_width + 2 * kv_width
    zs = _in_proj(h1b, w_in_t, row0=0, n=d_inner, tm=tm_mm, tn=1024, weight_buffers=2, name="in_proj_z",
                  apply_silu=True)
    xbc = _in_proj(h1b, w_in_t, row0=d_inner, n=xbc_width, tm=tm_mm, tn=1024, weight_buffers=2,
                   name="in_proj_xbc")
    qkv = _in_proj(h1b, w_in_t, row0=dt_hi, n=n_qkv, tm=tm_mm, tn=n_qkv // 2, weight_buffers=2,
                   name="in_proj_qkv")
    gates = _in_proj(h1b, w_in_t, row0=dt_hi + n_qkv, n=2 * d, tm=tm_mm, tn=1024, weight_buffers=2,
                     name="in_proj_gates")
    dt_t = _dt_proj(h1b, w_in_t, row0=dt_lo, n_heads=n_ssd_heads, tm=tm_mm)

    y_ssd = _ssd(zs, xbc, dt_t, conv_w, conv_b, dt_bias, a_log, d_skip, ssd_norm_g,
                 d_inner=d_inner, rows=min(SSD_ROWS_PER_STEP, t))
    y_attn = _swa(qkv, positions, attn_sinks, q_col=0, q_width=q_width,
                  k_col=q_width, v_col=q_width + kv_width, kv_width=kv_width)

    merged = _merge(y_ssd, y_attn, w_ssd_o, w_attn_o, gates, gs_col=0, ga_col=d, tm=tm_ffn, tn=512)
    h2 = _proj_ln(merged, w_out, h1, vec(ln2_g), vec(ln2_b), tm=tm_ffn)

    (out,) = _ffn_ln(h2, wg2, wu2, wd2, vec(ln3_g), vec(ln3_b), tm=tm_ffn, th=512, emit_bf16=False, name="ffn2_ln")
    return out


def kernel(x, positions, ffn1_w_gate, ffn1_w_up, ffn1_w_down, ln1_g, ln1_b, w_in, conv_w, conv_b, dt_bias, a_log, d_skip, ssd_norm_g, w_ssd_o, attn_sinks, w_attn_o, w_out, ln2_g, ln2_b, ffn2_w_gate, ffn2_w_up, ffn2_w_down, ln3_g, ln3_b):
    batch, depth = x.shape[0], ffn1_w_gate.shape[0]
    assert depth == DEPTH
    outs = []
    for bi in range(batch):
        h = x[bi]
        for l in range(depth):
            h = _layer(h, positions[bi], ffn1_w_gate[l], ffn1_w_up[l], ffn1_w_down[l], ln1_g[l], ln1_b[l],
                       w_in[l], conv_w[l], conv_b[l], dt_bias[l], a_log[l], d_skip[l], ssd_norm_g[l], w_ssd_o[l],
                       attn_sinks[l], w_attn_o[l], w_out[l], ln2_g[l], ln2_b[l],
                       ffn2_w_gate[l], ffn2_w_up[l], ffn2_w_down[l], ln3_g[l], ln3_b[l])
        outs.append(h)
    return jnp.stack(outs, axis=0)
```
